```python
import math
import jax
import jax.numpy as jnp
from jax import lax
import numpy as np

D_MODEL = 2048
BATCH = 8
SEQ = 4096
DEPTH = 4

N_MIXERS = 4
GROUP_WIDTH = D_MODEL // N_MIXERS
HEAD_DIM = 128
N_HEADS = GROUP_WIDTH // HEAD_DIM
MIX_WIDTH = N_MIXERS * GROUP_WIDTH
Q_LORA = 512
KV_LORA = 512
QK_NOPE = 128
QK_ROPE = 64
V_HEAD = HEAD_DIM
MLA_QK_DIM = QK_NOPE + QK_ROPE
DILATED_PAIRS = ((128, 1), (512, 4), (2048, 16))
FORGET_BIAS_INIT = 2.0
BLOCK = 128
ROPE_THETA = 10000.0
FFN_HIDDEN = -(-8 * D_MODEL // (3 * 256)) * 256
EPS = 1e-6
NEG_INF = -1e30
IN_SPLITS = ((Q_LORA, KV_LORA, QK_ROPE)
             + (GROUP_WIDTH,) * 3
             + (GROUP_WIDTH,) * 3 + (N_HEADS,)
             + (GROUP_WIDTH,) * 3)
IN_WIDTH = sum(IN_SPLITS)

kernel_name = "hybrid_parallel_heads_mla_dilated_fox_stickbreak"


def rms_norm(x, gain):
    xf = x.astype(jnp.float32)
    y = xf * lax.rsqrt(jnp.mean(jnp.square(xf), axis=-1, keepdims=True) + EPS)
    return (y * gain.astype(jnp.float32)).astype(x.dtype)


def rope_tables(seq, dim):
    pos = jnp.arange(seq, dtype=jnp.float32)
    inv_freq = ROPE_THETA ** (-jnp.arange(0, dim, 2, dtype=jnp.float32) / dim)
    ang = pos[:, None] * inv_freq[None, :]
    return jnp.cos(ang), jnp.sin(ang)


def apply_rope(x, cos, sin):
    xf = x.astype(jnp.float32)
    x1, x2 = jnp.split(xf, 2, axis=-1)
    c, s = cos[:, None, :], sin[:, None, :]
    return jnp.concatenate([x1 * c - x2 * s, x1 * s + x2 * c], axis=-1).astype(x.dtype)


def split_heads(t):
    b, s, _ = t.shape
    return t.reshape(b, s, N_HEADS, -1)


def to_query_blocks(t):
    b, h, s = t.shape[:3]
    t = t.reshape((b, h, s // BLOCK, BLOCK) + t.shape[3:])
    return jnp.moveaxis(t, 2, 0)


def from_query_blocks(o):
    nb, b, h, _, d = o.shape
    return jnp.moveaxis(o, 0, 2).reshape(b, h, nb * BLOCK, d).transpose(0, 2, 1, 3)


def causal_softmax_attention(q, k, v, scale, cum_log_forget=None):
    s_len = q.shape[1]
    qh, kh, vh = (t.transpose(0, 2, 1, 3) for t in (q, k, v))
    key_pos = jnp.arange(s_len)
    block_idx = jnp.arange(s_len // BLOCK)
    if cum_log_forget is None:
        xs = (to_query_blocks(qh), block_idx)
    else:
        xs = (to_query_blocks(qh), block_idx, to_query_blocks(cum_log_forget))

    def body(args):
        qi, i = args[0], args[1]
        s = jnp.einsum('bhqd,bhkd->bhqk', qi, kh).astype(jnp.float32) * scale
        if cum_log_forget is not None:
            s = s + args[2][..., None] - cum_log_forget[:, :, None, :]
        q_pos = i * BLOCK + jnp.arange(BLOCK)
        s = jnp.where(key_pos[None, :] <= q_pos[:, None], s, NEG_INF)
        p = jax.nn.softmax(s, axis=-1).astype(vh.dtype)
        return jnp.einsum('bhqk,bhkd->bhqd', p, vh)

    return from_query_blocks(lax.map(body, xs))


def mla_attention(q_lat, kv_lat, k_rope, q_norm, w_uq, kv_norm, w_ukv, cos, sin):
    b, s, _ = q_lat.shape
    q = (rms_norm(q_lat, q_norm) @ w_uq).reshape(b, s, N_HEADS, MLA_QK_DIM)
    q = jnp.concatenate([q[..., :QK_NOPE], apply_rope(q[..., QK_NOPE:], cos, sin)], axis=-1)
    kv = (rms_norm(kv_lat, kv_norm) @ w_ukv).reshape(b, s, N_HEADS, QK_NOPE + V_HEAD)
    k_nope, v = kv[..., :QK_NOPE], kv[..., QK_NOPE:]
    k_pe = apply_rope(k_rope[:, :, None, :], cos, sin)
    k = jnp.concatenate([k_nope, jnp.broadcast_to(k_pe, (b, s, N_HEADS, QK_ROPE))], axis=-1)
    return causal_softmax_attention(q, k, v, MLA_QK_DIM ** -0.5)


def banded_window_attention(q, k, v, steps, scale):
    n, h, l, d = q.shape
    pad = (-l) % BLOCK
    cfg = ((0, 0), (0, 0), (0, pad), (0, 0))
    q, k, v = jnp.pad(q, cfg), jnp.pad(k, cfg), jnp.pad(v, cfg)
    nb = (l + pad) // BLOCK
    qb = q.reshape(n, h, nb, BLOCK, d)

    def with_prev(t):
        tb = t.reshape(n, h, nb, BLOCK, d)
        prev = jnp.concatenate([jnp.zeros_like(tb[:, :, :1]), tb[:, :, :-1]], axis=2)
        return jnp.concatenate([prev, tb], axis=3)

    kb, vb = with_prev(k), with_prev(v)
    s = jnp.einsum('nhbqd,nhbkd->nhbqk', qb, kb).astype(jnp.float32) * scale
    q_idx = jnp.arange(BLOCK)
    k_idx = jnp.arange(2 * BLOCK)
    dist = BLOCK + q_idx[:, None] - k_idx[None, :]
    key_pos = (jnp.arange(nb)[:, None] - 1) * BLOCK + k_idx[None, :]
    valid = ((dist >= 0) & (dist <= steps))[None] & (key_pos >= 0)[:, None, :]
    s = jnp.where(valid, s, NEG_INF)
    lse = jax.nn.logsumexp(s, axis=-1)
    p = jnp.exp(s - lse[..., None]).astype(v.dtype)
    out = jnp.einsum('nhbqk,nhbkd->nhbqd', p, vb).reshape(n, h, nb * BLOCK, d)[:, :, :l]
    return out, lse.reshape(n, h, nb * BLOCK)[:, :, :l]


def dilated_window_attention(q, k, v):
    b, s, h, d = q.shape
    scale = d ** -0.5
    outs, lses = [], []
    for window, dilation in DILATED_PAIRS:
        l = s // dilation

        def by_residue(t):
            return t.reshape(b, l, dilation, h, d).transpose(0, 2, 3, 1, 4).reshape(b * dilation, h, l, d)

        o, lse = banded_window_attention(by_residue(q), by_residue(k), by_residue(v),
                                         window // dilation, scale)
        outs.append(o.reshape(b, dilation, h, l, d).transpose(0, 3, 1, 2, 4).reshape(b, s, h, d))
        lses.append(lse.reshape(b, dilation, h, l).transpose(0, 3, 1, 2).reshape(b, s, h))
    weights = jax.nn.softmax(jnp.stack(lses), axis=0).astype(q.dtype)
    return jnp.einsum('gbsh,gbshd->bshd', weights, jnp.stack(outs))


def forgetting_attention(q, k, v, f_logit, f_bias):
    log_f = jax.nn.log_sigmoid((f_logit + f_bias).astype(jnp.float32))
    cum = jnp.cumsum(log_f, axis=1).transpose(0, 2, 1)
    return causal_softmax_attention(q, k, v, HEAD_DIM ** -0.5, cum)


def stick_breaking_attention(q, k, v):
    s_len, d = q.shape[1], q.shape[-1]
    scale = d ** -0.5
    qh, kh, vh = (t.transpose(0, 2, 1, 3) for t in (q, k, v))
    key_pos = jnp.arange(s_len)

    def body(args):
        qi, i = args
        z = jnp.einsum('bhqd,bhkd->bhqk', qi, kh).astype(jnp.float32) * scale
        q_pos = i * BLOCK + jnp.arange(BLOCK)
        past = key_pos[None, :] < q_pos[:, None]
        log_keep = jnp.where(past, jax.nn.log_sigmoid(-z), 0.0)
        between = lax.cumsum(log_keep, axis=3, reverse=True) - log_keep
        a = jnp.where(past, jnp.exp(jax.nn.log_sigmoid(z) + between), 0.0)
        return jnp.einsum('bhqk,bhkd->bhqd', a.astype(vh.dtype), vh)

    xs = (to_query_blocks(qh), jnp.arange(s_len // BLOCK))
    return from_query_blocks(lax.map(body, xs))


def hybrid_mixer(h, w_in, mla_q_norm, w_uq, mla_kv_norm, w_ukv, fox_forget_bias,
                 group_norm, w_out, rope_full, rope_mla):
    b, s, _ = h.shape
    proj = h @ w_in
    offsets = np.cumsum(IN_SPLITS)[:-1].tolist()
    (q_lat, kv_lat, k_rope, q_b, k_b, v_b, q_c, k_c, v_c, f_c, q_d, k_d, v_d) = \
        jnp.split(proj, offsets, axis=-1)
    cos, sin = rope_full
    out_a = mla_attention(q_lat, kv_lat, k_rope, mla_q_norm, w_uq, mla_kv_norm, w_ukv, *rope_mla)
    out_b = dilated_window_attention(apply_rope(split_heads(q_b), cos, sin),
                                     apply_rope(split_heads(k_b), cos, sin), split_heads(v_b))
    out_c = forgetting_attention(split_heads(q_c), split_heads(k_c), split_heads(v_c),
                                 f_c, fox_forget_bias)
    out_d = stick_breaking_attention(split_heads(q_d), split_heads(k_d), split_heads(v_d))
    groups = jnp.stack([o.reshape(b, s, GROUP_WIDTH) for o in (out_a, out_b, out_c, out_d)],
                       axis=2)
    groups = rms_norm(groups, group_norm.reshape(N_MIXERS, GROUP_WIDTH))
    return groups.reshape(b, s, MIX_WIDTH) @ w_out


def setup_inputs(seed: int = 0) -> dict:
    key = jax.random.key(seed)
    ks = jax.random.split(key, 16)

    def w(k, shape, fan_in):
        return jax.random.normal(k, shape, jnp.float32) * fan_in ** -0.5

    def gain(k, shape):
        return 1.0 + 0.05 * jax.random.normal(k, shape, jnp.float32)

    return {
        "x": jax.random.normal(ks[0], (BATCH, SEQ, D_MODEL), jnp.float32),
        "attn_norm": gain(ks[1], (DEPTH, D_MODEL)),
        "w_in": w(ks[2], (DEPTH, D_MODEL, IN_WIDTH), D_MODEL),
        "mla_q_norm": gain(ks[3], (DEPTH, Q_LORA)),
        "w_uq": w(ks[4], (DEPTH, Q_LORA, N_HEADS * MLA_QK_DIM), Q_LORA),
        "mla_kv_norm": gain(ks[5], (DEPTH, KV_LORA)),
        "w_ukv": w(ks[6], (DEPTH, KV_LORA, N_HEADS * (QK_NOPE + V_HEAD)), KV_LORA),
        "fox_forget_bias": FORGET_BIAS_INIT + 0.1 * jax.random.normal(ks[7], (DEPTH, N_HEADS), jnp.float32),
        "group_norm": gain(ks[8], (DEPTH, MIX_WIDTH)),
        "w_out": w(ks[9], (DEPTH, MIX_WIDTH, D_MODEL), MIX_WIDTH),
        "ffn_norm": gain(ks[10], (DEPTH, D_MODEL)),
        "w_gate": w(ks[11], (DEPTH, D_MODEL, FFN_HIDDEN), D_MODEL),
        "w_up": w(ks[12], (DEPTH, D_MODEL, FFN_HIDDEN), D_MODEL),
        "w_down": w(ks[13], (DEPTH, FFN_HIDDEN, D_MODEL), FFN_HIDDEN),
        "final_norm": gain(ks[14], (D_MODEL,)),
    }


def reference(x, attn_norm, w_in, mla_q_norm, w_uq, mla_kv_norm, w_ukv, fox_forget_bias,
              group_norm, w_out, ffn_norm, w_gate, w_up, w_down, final_norm):
    s_len = x.shape[1]
    rope_full = rope_tables(s_len, HEAD_DIM)
    rope_mla = rope_tables(s_len, QK_ROPE)
    for l in range(DEPTH):
        x = x + hybrid_mixer(rms_norm(x, attn_norm[l]), w_in[l], mla_q_norm[l], w_uq[l],
                             mla_kv_norm[l], w_ukv[l], fox_forget_bias[l], group_norm[l],
                             w_out[l], rope_full, rope_mla)
        h = rms_norm(x, ffn_norm[l])
        x = x + (jax.nn.silu(h @ w_gate[l]) * (h @ w_up[l])) @ w_down[l]
    return rms_norm(x, final_norm)
```

```python
import functools

import jax
import jax.numpy as jnp
from jax import lax
from jax.experimental import pallas as pl
from jax.experimental.pallas import tpu as pltpu

F32 = jnp.float32
BF16 = jnp.bfloat16

D_MODEL = 2048
DEPTH = 4
N_HEADS = 4
HEAD_DIM = 128
GROUP_WIDTH = N_HEADS * HEAD_DIM
Q_LORA = 512
KV_LORA = 512
QK_NOPE = 128
QK_ROPE = 64
MLA_QK_DIM = QK_NOPE + QK_ROPE
MLA_HEAD_PAD = 2 * HEAD_DIM
DILATED_PAIRS = ((128, 1), (512, 4), (2048, 16))
BAND = 128
ROPE_THETA = 10000.0
FFN_HIDDEN = 5632
EPS = 1e-6
NEG_INF = -1e30

MAIN_WIDTH = Q_LORA + KV_LORA + 9 * GROUP_WIDTH
TAIL_WIDTH = 2 * HEAD_DIM
MAIN_BLOCKS = MAIN_WIDTH // HEAD_DIM
COL_QB, COL_KB, COL_VB = 8, 12, 16
COL_QC, COL_KC, COL_VC = 20, 24, 28
COL_QD, COL_KD, COL_VD = 32, 36, 40

VMEM_LIMIT_BYTES = 56 * 1024 * 1024
TM_IN, TN_IN = 1024, 512
TM_MLA = 1024
TM_OUT = 512
TM_FFN, TF_FFN = 512, 512
TQ_ATT = 512
T_SB = 256
CUM_CHUNK = 256


def _params(*sem):
    return pltpu.CompilerParams(dimension_semantics=sem, vmem_limit_bytes=VMEM_LIMIT_BYTES)


def _rms(x, gain):
    return (x * lax.rsqrt(jnp.mean(x * x, axis=-1, keepdims=True) + EPS)) * gain


def _dot(a, b):
    return jnp.dot(a, b, preferred_element_type=F32)


def _dot_nt(a, b):
    return lax.dot_general(a, b, (((1,), (1,)), ((), ())), preferred_element_type=F32)


def _rope128(y, cos, sin):
    return y * cos + pltpu.roll(y, 64, 1) * sin


def _in_proj_kernel(x_ref, g_ref, w_ref, wt_ref, o_ref, ot_ref, h_ref):
    @pl.when(pl.program_id(1) == 0)
    def _():
        hb = _rms(x_ref[...], g_ref[...]).astype(BF16)
        h_ref[...] = hb
        ot_ref[...] = _dot(hb, wt_ref[...])

    o_ref[...] = _dot(h_ref[...], w_ref[...]).astype(o_ref.dtype)


def _in_proj(x, gain, w_main, w_tail):
    m = x.shape[0]
    return pl.pallas_call(
        _in_proj_kernel,
        grid=(m // TM_IN, MAIN_WIDTH // TN_IN),
        in_specs=[
            pl.BlockSpec((TM_IN, D_MODEL), lambda i, j: (i, 0)),
            pl.BlockSpec((1, D_MODEL), lambda i, j: (0, 0)),
            pl.BlockSpec((D_MODEL, TN_IN), lambda i, j: (0, j)),
            pl.BlockSpec((D_MODEL, TAIL_WIDTH), lambda i, j: (0, 0)),
        ],
        out_specs=[
            pl.BlockSpec((TM_IN, TN_IN), lambda i, j: (i, j)),
            pl.BlockSpec((TM_IN, TAIL_WIDTH), lambda i, j: (i, 0)),
        ],
        out_shape=[
            jax.ShapeDtypeStruct((m, MAIN_WIDTH), BF16),
            jax.ShapeDtypeStruct((m, TAIL_WIDTH), F32),
        ],
        scratch_shapes=[pltpu.VMEM((TM_IN, D_MODEL), BF16)],
        compiler_params=_params("parallel", "arbitrary"),
        name="in_proj",
    )(x, gain, w_main, w_tail)


def _mla_prep_kernel(ql_ref, kvl_ref, t_ref, gq_ref, gkv_ref, wq_ref, wkv_ref, cos_ref, sin_ref,
                     q_ref, kv_ref, kpe_ref):
    cos = cos_ref[...]
    sin = sin_ref[...]
    hq = _rms(ql_ref[...].astype(F32), gq_ref[...]).astype(BF16)
    q = _dot(hq, wq_ref[...])
    for h in range(N_HEADS):
        lo = h * MLA_HEAD_PAD
        q_ref[:, lo:lo + HEAD_DIM] = q[:, lo:lo + HEAD_DIM].astype(BF16)
        y = q[:, lo + HEAD_DIM:lo + MLA_HEAD_PAD]
        q_ref[:, lo + HEAD_DIM:lo + MLA_HEAD_PAD] = _rope128(y, cos, sin).astype(BF16)
    hkv = _rms(kvl_ref[...].astype(F32), gkv_ref[...]).astype(BF16)
    kv_ref[...] = _dot(hkv, wkv_ref[...]).astype(BF16)
    kpe_ref[...] = _rope128(t_ref[...], cos, sin).astype(BF16)


def _mla_prep(proj, tail, gq, gkv, wq, wkv, cos_t, sin_t, seq):
    m = proj.shape[0]
    nseq = seq // TM_MLA
    return pl.pallas_call(
        _mla_prep_kernel,
        grid=(m // TM_MLA,),
        in_specs=[
            pl.BlockSpec((TM_MLA, Q_LORA), lambda i: (i, 0)),
            pl.BlockSpec((TM_MLA, KV_LORA), lambda i: (i, 1)),
            pl.BlockSpec((TM_MLA, HEAD_DIM), lambda i: (i, 0)),
            pl.BlockSpec((1, Q_LORA), lambda i: (0, 0)),
            pl.BlockSpec((1, KV_LORA), lambda i: (0, 0)),
            pl.BlockSpec((Q_LORA, N_HEADS * MLA_HEAD_PAD), lambda i: (0, 0)),
            pl.BlockSpec((KV_LORA, 2 * GROUP_WIDTH), lambda i: (0, 0)),
            pl.BlockSpec((TM_MLA, HEAD_DIM), lambda i: (i % nseq, 0)),
            pl.BlockSpec((TM_MLA, HEAD_DIM), lambda i: (i % nseq, 0)),
        ],
        out_specs=[
            pl.BlockSpec((TM_MLA, N_HEADS * MLA_HEAD_PAD), lambda i: (i, 0)),
            pl.BlockSpec((TM_MLA, 2 * GROUP_WIDTH), lambda i: (i, 0)),
            pl.BlockSpec((TM_MLA, HEAD_DIM), lambda i: (i, 0)),
        ],
        out_shape=[
            jax.ShapeDtypeStruct((m, N_HEADS * MLA_HEAD_PAD), BF16),
            jax.ShapeDtypeStruct((m, 2 * GROUP_WIDTH), BF16),
            jax.ShapeDtypeStruct((m, HEAD_DIM), BF16),
        ],
        compiler_params=_params("parallel"),
        name="mla_prep",
    )(proj, proj, tail, gq, gkv, wq, wkv, cos_t, sin_t)


def _split3(x):
    hi = x.astype(BF16)
    r1 = x - hi.astype(F32)
    mid = r1.astype(BF16)
    lo = (r1 - mid.astype(F32)).astype(BF16)
    return hi, mid, lo


def _fox_prep_kernel(t_ref, b_ref, col_ref, row_ref, *, seq):
    r = lax.broadcasted_iota(jnp.int32, (CUM_CHUNK, CUM_CHUNK), 0)
    c = lax.broadcasted_iota(jnp.int32, (CUM_CHUNK, CUM_CHUNK), 1)
    lower = jnp.where(c <= r, 1.0, 0.0).astype(BF16)
    carry = jnp.zeros((1, HEAD_DIM), F32)
    for ci in range(seq // CUM_CHUNK):
        rows = slice(ci * CUM_CHUNK, (ci + 1) * CUM_CHUNK)
        x = t_ref[rows, :] + b_ref[...]
        log_f = jnp.minimum(x, 0.0) - jnp.log(1.0 + jnp.exp(-jnp.abs(x)))
        hi, mid, lo = _split3(log_f)
        cum = (_dot(lower, hi) + _dot(lower, mid)) + _dot(lower, lo) + carry
        col_ref[rows, :] = cum
        cum_t = cum.T
        for h in range(N_HEADS):
            row_ref[0, h, :, rows] = cum_t[h:h + 1, :]
        carry = cum[CUM_CHUNK - 1:CUM_CHUNK, :]


def _fox_prep(tail, bias_row, batch, seq):
    m = tail.shape[0]
    return pl.pallas_call(
        functools.partial(_fox_prep_kernel, seq=seq),
        grid=(batch,),
        in_specs=[
            pl.BlockSpec((seq, HEAD_DIM), lambda b: (b, 1)),
            pl.BlockSpec((1, HEAD_DIM), lambda b: (0, 0)),
        ],
        out_specs=[
            pl.BlockSpec((seq, HEAD_DIM), lambda b: (b, 0)),
            pl.BlockSpec((1, N_HEADS, 1, seq), lambda b: (b, 0, 0, 0)),
        ],
        out_shape=[
            jax.ShapeDtypeStruct((m, HEAD_DIM), F32),
            jax.ShapeDtypeStruct((batch, N_HEADS, 1, seq), F32),
        ],
        compiler_params=_params("parallel"),
        name="fox_prep",
    )(tail, bias_row)


def _flash_kernel(*refs, scale, mla, fox):
    if mla:
        q_ref, k_ref, kpe_ref, v_ref, o_ref, m_s, l_s, acc_s = refs
    elif fox:
        q_ref, k_ref, v_ref, ccol_ref, crow_ref, o_ref, m_s, l_s, acc_s = refs
    else:
        q_ref, k_ref, v_ref, o_ref, m_s, l_s, acc_s = refs
    t = TQ_ATT
    head = pl.program_id(1)
    qi = pl.program_id(2)
    q = q_ref[...]
    if fox:
        lane = lax.broadcasted_iota(jnp.int32, (t, HEAD_DIM), 1)
        c_t = jnp.sum(jnp.where(lane == head, ccol_ref[...], 0.0), axis=1, keepdims=True)
    m_s[...] = jnp.full((t, 1), NEG_INF, F32)
    l_s[...] = jnp.zeros((t, 1), F32)
    acc_s[...] = jnp.zeros((t, HEAD_DIM), F32)

    def chunk(start, diagonal):
        k = k_ref[pl.ds(start, t), :]
        if mla:
            k = jnp.concatenate([k, kpe_ref[pl.ds(start, t), :]], axis=1)
        s = _dot_nt(q, k) * scale
        if fox:
            s = (s + c_t) - crow_ref[0, 0, :, pl.ds(start, t)]
        if diagonal:
            row = lax.broadcasted_iota(jnp.int32, (t, t), 0)
            col = lax.broadcasted_iota(jnp.int32, (t, t), 1)
            s = jnp.where(col <= row, s, NEG_INF)
        m_prev = m_s[...]
        m_new = jnp.maximum(m_prev, jnp.max(s, axis=-1, keepdims=True))
        alpha = jnp.exp(m_prev - m_new)
        p = jnp.exp(s - m_new)
        l_s[...] = alpha * l_s[...] + jnp.sum(p, axis=-1, keepdims=True)
        acc_s[...] = alpha * acc_s[...] + _dot(p.astype(BF16), v_ref[pl.ds(start, t), :])
        m_s[...] = m_new

    def body(c, carry):
        chunk(pl.multiple_of(c * t, t), False)
        return carry

    lax.fori_loop(0, qi, body, 0)
    chunk(pl.multiple_of(qi * t, t), True)
    o_ref[...] = (acc_s[...] / l_s[...]).astype(o_ref.dtype)


def _flash_scratch():
    return [pltpu.VMEM((TQ_ATT, 1), F32), pltpu.VMEM((TQ_ATT, 1), F32),
            pltpu.VMEM((TQ_ATT, HEAD_DIM), F32)]


def _mla_attention(q_a, kv_a, kpe, batch, seq):
    m = q_a.shape[0]
    nq = seq // TQ_ATT
    return pl.pallas_call(
        functools.partial(_flash_kernel, scale=MLA_QK_DIM ** -0.5, mla=True, fox=False),
        grid=(batch, N_HEADS, nq),
        in_specs=[
            pl.BlockSpec((TQ_ATT, MLA_HEAD_PAD), lambda b, h, i: (b * nq + i, h)),
            pl.BlockSpec((seq, HEAD_DIM), lambda b, h, i: (b, h)),
            pl.BlockSpec((seq, HEAD_DIM), lambda b, h, i: (b, 0)),
            pl.BlockSpec((seq, HEAD_DIM), lambda b, h, i: (b, N_HEADS + h)),
        ],
        out_specs=pl.BlockSpec((TQ_ATT, HEAD_DIM), lambda b, h, i: (b * nq + i, h)),
        out_shape=jax.ShapeDtypeStruct((m, GROUP_WIDTH), BF16),
        scratch_shapes=_flash_scratch(),
        compiler_params=_params("parallel", "parallel", "arbitrary"),
        name="mla_attention",
    )(q_a, kv_a, kpe, kv_a)


def _fox_attention(proj, ccol, crow, batch, seq):
    m = proj.shape[0]
    nq = seq // TQ_ATT
    return pl.pallas_call(
        functools.partial(_flash_kernel, scale=HEAD_DIM ** -0.5, mla=False, fox=True),
        grid=(batch, N_HEADS, nq),
        in_specs=[
            pl.BlockSpec((TQ_ATT, HEAD_DIM), lambda b, h, i: (b * nq + i, COL_QC + h)),
            pl.BlockSpec((seq, HEAD_DIM), lambda b, h, i: (b, COL_KC + h)),
            pl.BlockSpec((seq, HEAD_DIM), lambda b, h, i: (b, COL_VC + h)),
            pl.BlockSpec((TQ_ATT, HEAD_DIM), lambda b, h, i: (b * nq + i, 0)),
            pl.BlockSpec((1, 1, 1, seq), lambda b, h, i: (b, h, 0, 0)),
        ],
        out_specs=pl.BlockSpec((TQ_ATT, HEAD_DIM), lambda b, h, i: (b * nq + i, h)),
        out_shape=jax.ShapeDtypeStruct((m, GROUP_WIDTH), BF16),
        scratch_shapes=_flash_scratch(),
        compiler_params=_params("parallel", "parallel", "arbitrary"),
        name="fox_attention",
    )(proj, proj, proj, ccol, crow)


def _sb_kernel(q_ref, k_ref, v_ref, o_ref, r_s, acc_s, *, scale):
    t = T_SB
    qi = pl.program_id(2)
    q = q_ref[...]
    row = lax.broadcasted_iota(jnp.int32, (t, t), 0)
    col = lax.broadcasted_iota(jnp.int32, (t, t), 1)
    later = jnp.where(row > col, 1.0, 0.0).astype(BF16)
    r_s[...] = jnp.zeros((t, 1), F32)
    acc_s[...] = jnp.zeros((t, HEAD_DIM), F32)

    def chunk(start, diagonal):
        z = _dot_nt(q, k_ref[pl.ds(start, t), :]) * scale
        soft = jnp.log(1.0 + jnp.exp(-jnp.abs(z)))
        log_beta = jnp.minimum(z, 0.0) - soft
        log_keep = -jnp.maximum(z, 0.0) - soft
        if diagonal:
            past = col < row
            log_keep = jnp.where(past, log_keep, 0.0)
        hi = log_keep.astype(BF16)
        lo = (log_keep - hi.astype(F32)).astype(BF16)
        between = (_dot(hi, later) + _dot(lo, later)) + r_s[...]
        a = jnp.exp(log_beta + between)
        if diagonal:
            a = jnp.where(past, a, 0.0)
        acc_s[...] += _dot(a.astype(BF16), v_ref[pl.ds(start, t), :])
        r_s[...] += jnp.sum(log_keep, axis=-1, keepdims=True)

    chunk(pl.multiple_of(qi * t, t), True)

    def body(j, carry):
        chunk(pl.multiple_of((qi - 1 - j) * t, t), False)
        return carry

    lax.fori_loop(0, qi, body, 0)
    o_ref[...] = acc_s[...].astype(o_ref.dtype)


def _sb_attention(proj, batch, seq):
    m = proj.shape[0]
    nq = seq // T_SB
    return pl.pallas_call(
        functools.partial(_sb_kernel, scale=HEAD_DIM ** -0.5),
        grid=(batch, N_HEADS, nq),
        in_specs=[
            pl.BlockSpec((T_SB, HEAD_DIM), lambda b, h, i: (b * nq + i, COL_QD + h)),
            pl.BlockSpec((seq, HEAD_DIM), lambda b, h, i: (b, COL_KD + h)),
            pl.BlockSpec((seq, HEAD_DIM), lambda b, h, i: (b, COL_VD + h)),
        ],
        out_specs=pl.BlockSpec((T_SB, HEAD_DIM), lambda b, h, i: (b * nq + i, h)),
        out_shape=jax.ShapeDtypeStruct((m, GROUP_WIDTH), BF16),
        scratch_shapes=[pltpu.VMEM((T_SB, 1), F32), pltpu.VMEM((T_SB, HEAD_DIM), F32)],
        compiler_params=_params("parallel", "parallel", "arbitrary"),
        name="sb_attention",
    )(proj, proj, proj)


def _band_kernel(q_ref, k_ref, v_ref, cos_ref, sin_ref, o_ref, lse_ref, q_s, k_s, *, length, scale):
    cos = cos_ref[...]
    sin = sin_ref[...]
    q_s[...] = _rope128(q_ref[...].astype(F32), cos, sin).astype(BF16)
    k_s[...] = _rope128(k_ref[...].astype(F32), cos, sin).astype(BF16)
    eye = (lax.broadcasted_iota(jnp.int32, (BAND, BAND), 0)
           == lax.broadcasted_iota(jnp.int32, (BAND, BAND), 1))

    def attend(q_start, k_start, n_keys, valid):
        s = _dot_nt(q_s[pl.ds(q_start, BAND), :], k_s[pl.ds(k_start, n_keys), :]) * scale
        s = jnp.where(valid, s, NEG_INF)
        m = jnp.max(s, axis=-1, keepdims=True)
        p = jnp.exp(s - m)
        l = jnp.sum(p, axis=-1, keepdims=True)
        o = _dot(p.astype(BF16), v_ref[pl.ds(k_start, n_keys), :]) / l
        o_ref[pl.ds(q_start, BAND), :] = o.astype(o_ref.dtype)
        lse = m + jnp.log(l)
        lse_ref[0, 0, :, pl.ds(q_start, BAND)] = jnp.sum(
            jnp.where(eye, lse, 0.0), axis=0, keepdims=True)

    row1 = lax.broadcasted_iota(jnp.int32, (BAND, BAND), 0)
    col1 = lax.broadcasted_iota(jnp.int32, (BAND, BAND), 1)
    attend(0, 0, BAND, col1 <= row1)

    row2 = lax.broadcasted_iota(jnp.int32, (BAND, 2 * BAND), 0)
    col2 = lax.broadcasted_iota(jnp.int32, (BAND, 2 * BAND), 1)
    dist = BAND + row2 - col2
    valid2 = (dist >= 0) & (dist <= BAND)

    def body(i, carry):
        attend(pl.multiple_of(i * BAND, BAND), pl.multiple_of((i - 1) * BAND, BAND), 2 * BAND, valid2)
        return carry

    lax.fori_loop(1, length // BAND, body, 0)


def _band_attention(proj, cos_b, sin_b, dilation, batch, seq):
    m = proj.shape[0]
    length = seq // dilation
    view = proj.reshape(m // dilation, dilation * MAIN_WIDTH)
    cos_v = cos_b.reshape(length, dilation * HEAD_DIM)
    sin_v = sin_b.reshape(length, dilation * HEAD_DIM)
    blk = (length, HEAD_DIM)
    o, lse = pl.pallas_call(
        functools.partial(_band_kernel, length=length, scale=HEAD_DIM ** -0.5),
        grid=(batch, N_HEADS, dilation),
        in_specs=[
            pl.BlockSpec(blk, lambda b, h, r: (b, r * MAIN_BLOCKS + COL_QB + h)),
            pl.BlockSpec(blk, lambda b, h, r: (b, r * MAIN_BLOCKS + COL_KB + h)),
            pl.BlockSpec(blk, lambda b, h, r: (b, r * MAIN_BLOCKS + COL_VB + h)),
            pl.BlockSpec(blk, lambda b, h, r: (0, r)),
            pl.BlockSpec(blk, lambda b, h, r: (0, r)),
        ],
        out_specs=[
            pl.BlockSpec(blk, lambda b, h, r: (b, r * N_HEADS + h)),
            pl.BlockSpec((1, 1, 1, length), lambda b, h, r: (b, r * N_HEADS + h, 0, 0)),
        ],
        out_shape=[
            jax.ShapeDtypeStruct((m // dilation, dilation * GROUP_WIDTH), BF16),
            jax.ShapeDtypeStruct((batch, dilation * N_HEADS, 1, length), F32),
        ],
        scratch_shapes=[pltpu.VMEM(blk, BF16), pltpu.VMEM(blk, BF16)],
        compiler_params=_params("parallel", "parallel", "arbitrary"),
        name=f"band_attention_d{dilation}",
    )(view, view, view, cos_v, sin_v)
    o = o.reshape(m, GROUP_WIDTH)
    lse = lse.reshape(batch, dilation, N_HEADS, length).transpose(0, 3, 1, 2).reshape(m, N_HEADS)
    return o, lse


def _out_proj_kernel(oa_ref, ob1_ref, ob2_ref, ob3_ref, lse_ref, oc_ref, od_ref, gn_ref, w_ref, x_ref,
                     o_ref, g_s):
    def put(group, val):
        cols = slice(group * GROUP_WIDTH, (group + 1) * GROUP_WIDTH)
        g_s[:, cols] = _rms(val, gn_ref[:, cols]).astype(BF16)

    put(0, oa_ref[...].astype(F32))
    lse = lse_ref[...]
    heads = []
    for h in range(N_HEADS):
        l1, l2, l3 = (lse[:, g * N_HEADS + h:g * N_HEADS + h + 1] for g in range(3))
        top = jnp.maximum(jnp.maximum(l1, l2), l3)
        e1, e2, e3 = jnp.exp(l1 - top), jnp.exp(l2 - top), jnp.exp(l3 - top)
        den = (e1 + e2) + e3
        cols = slice(h * HEAD_DIM, (h + 1) * HEAD_DIM)
        heads.append(((e1 / den) * ob1_ref[:, cols].astype(F32)
                      + (e2 / den) * ob2_ref[:, cols].astype(F32))
                     + (e3 / den) * ob3_ref[:, cols].astype(F32))
    put(1, jnp.concatenate(heads, axis=1))
    put(2, oc_ref[...].astype(F32))
    put(3, od_ref[...].astype(F32))
    o_ref[...] = x_ref[...] + _dot(g_s[...], w_ref[...])


def _out_proj(oa, ob, lse, oc, od, gn, w_out, x):
    m = x.shape[0]
    grp = pl.BlockSpec((TM_OUT, GROUP_WIDTH), lambda i: (i, 0))
    return pl.pallas_call(
        _out_proj_kernel,
        grid=(m // TM_OUT,),
        in_specs=[
            grp, grp, grp, grp,
            pl.BlockSpec((TM_OUT, 3 * N_HEADS), lambda i: (i, 0)),
            grp, grp,
            pl.BlockSpec((1, D_MODEL), lambda i: (0, 0)),
            pl.BlockSpec((D_MODEL, D_MODEL), lambda i: (0, 0)),
            pl.BlockSpec((TM_OUT, D_MODEL), lambda i: (i, 0)),
        ],
        out_specs=pl.BlockSpec((TM_OUT, D_MODEL), lambda i: (i, 0)),
        out_shape=jax.ShapeDtypeStruct((m, D_MODEL), F32),
        scratch_shapes=[pltpu.VMEM((TM_OUT, D_MODEL), BF16)],
        compiler_params=_params("parallel"),
        name="out_proj",
    )(oa, ob[0], ob[1], ob[2], lse, oc, od, gn, w_out, x)


def _ffn_kernel(x_ref, g_ref, wg_ref, wu_ref, wd_ref, fg_ref, o_ref, h_s, *, final):
    f = pl.program_id(1)

    @pl.when(f == 0)
    def _():
        x = x_ref[...]
        h_s[...] = _rms(x, g_ref[...]).astype(BF16)
        o_ref[...] = x

    h = h_s[...]
    gate = _dot(h, wg_ref[...])
    up = _dot(h, wu_ref[...])
    act = (gate / (1.0 + jnp.exp(-gate))) * up
    o_ref[...] += _dot(act.astype(BF16), wd_ref[...])

    if final:
        @pl.when(f == pl.num_programs(1) - 1)
        def _():
            o_ref[...] = _rms(o_ref[...], fg_ref[...])


def _ffn(x, gain, w_gate, w_up, w_down, final_gain, final):
    m = x.shape[0]
    return pl.pallas_call(
        functools.partial(_ffn_kernel, final=final),
        grid=(m // TM_FFN, FFN_HIDDEN // TF_FFN),
        in_specs=[
            pl.BlockSpec((TM_FFN, D_MODEL), lambda i, f: (i, 0)),
            pl.BlockSpec((1, D_MODEL), lambda i, f: (0, 0)),
            pl.BlockSpec((D_MODEL, TF_FFN), lambda i, f: (0, f)),
            pl.BlockSpec((D_MODEL, TF_FFN), lambda i, f: (0, f)),
            pl.BlockSpec((TF_FFN, D_MODEL), lambda i, f: (f, 0)),
            pl.BlockSpec((1, D_MODEL), lambda i, f: (0, 0)),
        ],
        out_specs=pl.BlockSpec((TM_FFN, D_MODEL), lambda i, f: (i, 0)),
        out_shape=jax.ShapeDtypeStruct((m, D_MODEL), F32),
        scratch_shapes=[pltpu.VMEM((TM_FFN, D_MODEL), BF16)],
        compiler_params=_params("parallel", "arbitrary"),
        name="ffn_final" if final else "ffn",
    )(x, gain, w_gate, w_up, w_down, final_gain)


def _rot_cols(w):
    half = w.shape[-1] // 2
    return jnp.concatenate([-w[..., half:], w[..., :half]], axis=-1)


def _prep_weights(w_in, w_uq, w_ukv, fox_forget_bias):
    depth = w_in.shape[0]
    k_rope = w_in[..., 1024:1088]
    w_main = jnp.concatenate([w_in[..., 0:1024], w_in[..., 1088:4160], w_in[..., 4164:5700]], axis=-1)
    w_tail = jnp.concatenate([k_rope, _rot_cols(k_rope), w_in[..., 4160:4164],
                              jnp.zeros((depth, D_MODEL, HEAD_DIM - N_HEADS), w_in.dtype)], axis=-1)
    uq = w_uq.reshape(depth, Q_LORA, N_HEADS, MLA_QK_DIM)
    pe = uq[..., QK_NOPE:]
    uq = jnp.concatenate([uq[..., :QK_NOPE], pe, _rot_cols(pe)], axis=-1)
    uq = uq.reshape(depth, Q_LORA, N_HEADS * MLA_HEAD_PAD)
    ukv = w_ukv.reshape(depth, KV_LORA, N_HEADS, 2, HEAD_DIM).transpose(0, 1, 3, 2, 4)
    ukv = ukv.reshape(depth, KV_LORA, 2 * GROUP_WIDTH)
    bias_rows = jnp.pad(fox_forget_bias, ((0, 0), (0, HEAD_DIM - N_HEADS)))[:, None, :]
    return w_main.astype(BF16), w_tail.astype(BF16), uq.astype(BF16), ukv.astype(BF16), bias_rows


def _rope_tables(seq):
    pos = jnp.arange(seq, dtype=F32)[:, None]

    def angles(dim):
        inv_freq = ROPE_THETA ** (-jnp.arange(0, dim, 2, dtype=F32) / dim)
        return pos * inv_freq[None, :]

    ang_b = angles(HEAD_DIM)
    cos_b = jnp.concatenate([jnp.cos(ang_b)] * 2, axis=1)
    sin_b = jnp.concatenate([-jnp.sin(ang_b), jnp.sin(ang_b)], axis=1)
    ang_a = angles(QK_ROPE)
    zeros = jnp.zeros((seq, HEAD_DIM - QK_ROPE), F32)
    cos_a = jnp.concatenate([jnp.cos(ang_a)] * 2 + [zeros], axis=1)
    sin_a = jnp.concatenate([jnp.sin(ang_a)] * 2 + [zeros], axis=1)
    return cos_a, sin_a, cos_b, sin_b


def kernel(x, attn_norm, w_in, mla_q_norm, w_uq, mla_kv_norm, w_ukv, fox_forget_bias, group_norm,
           w_out, ffn_norm, w_gate, w_up, w_down, final_norm):
    batch, seq, _ = x.shape
    m = batch * seq
    assert seq % (16 * BAND) == 0 and seq % TM_MLA == 0 and m % TM_IN == 0
    w_main, w_tail, uq, ukv, bias_rows = _prep_weights(w_in, w_uq, w_ukv, fox_forget_bias)
    w_out_b, w_gate_b, w_up_b, w_down_b = (w.astype(BF16) for w in (w_out, w_gate, w_up, w_down))
    cos_a, sin_a, cos_b, sin_b = _rope_tables(seq)
    row = lambda v: v.reshape(1, -1)

    xf = x.reshape(m, D_MODEL)
    for l in range(DEPTH):
        proj, tail = _in_proj(xf, row(attn_norm[l]), w_main[l], w_tail[l])
        q_a, kv_a, kpe = _mla_prep(proj, tail, row(mla_q_norm[l]), row(mla_kv_norm[l]), uq[l], ukv[l],
                                   cos_a, sin_a, seq)
        ccol, crow = _fox_prep(tail, bias_rows[l], batch, seq)
        out_a = _mla_attention(q_a, kv_a, kpe, batch, seq)
        out_c = _fox_attention(proj, ccol, crow, batch, seq)
        out_d = _sb_attention(proj, batch, seq)
        groups = [_band_attention(proj, cos_b, sin_b, d, batch, seq) for _, d in DILATED_PAIRS]
        lse = jnp.concatenate([g[1] for g in groups], axis=1)
        xf = _out_proj(out_a, [g[0] for g in groups], lse, out_c, out_d, row(group_norm[l]),
                       w_out_b[l], xf)
        xf = _ffn(xf, row(ffn_norm[l]), w_gate_b[l], w_up_b[l], w_down_b[l], row(final_norm),
                  final=(l == DEPTH - 1))
    return xf.reshape(batch, seq, D_MODEL)
```

```python
import functools
import math

import jax
import jax.numpy as jnp
from jax import lax
from jax.experimental import pallas as pl
from jax.experimental.pallas import tpu as pltpu

F32 = jnp.float32
BF16 = jnp.bfloat16

D_MODEL = 2048
DEPTH = 4
N_HEADS = 4
HEAD_DIM = 128
GROUP_WIDTH = N_HEADS * HEAD_DIM
Q_LORA = 512
KV_LORA = 512
QK_NOPE = 128
QK_ROPE = 64
MLA_QK_DIM = QK_NOPE + QK_ROPE
MLA_HEAD_PAD = 2 * HEAD_DIM
DILATED_PAIRS = ((128, 1), (512, 4), (2048, 16))
BAND = 128
assert all(window // dilation == BAND for window, dilation in DILATED_PAIRS)
DILATIONS = tuple(dilation for _, dilation in DILATED_PAIRS)
ROPE_THETA = 10000.0
FFN_HIDDEN = 5632
EPS = 1e-6
NEG_INF = -1e30
LOG2E = math.log2(math.e)

MAIN_WIDTH = Q_LORA + KV_LORA + 9 * GROUP_WIDTH
TAIL_WIDTH = 2 * HEAD_DIM
COL_QB, COL_KB, COL_VB = 8, 12, 16
COL_QC, COL_KC, COL_VC = 20, 24, 28
COL_QD, COL_KD, COL_VD = 32, 36, 40

VMEM_LIMIT_BYTES = 56 * 1024 * 1024
TM_IN, TN_IN = 1024, 512
TM_MLA = 1024
TM_OUT = 512
ROWS_OUT = 128
TM_FFN, TF_FFN = 512, 512
TQ_ATT, TK_ATT = 2048, 512
TQ_SB, T_SB = 1024, 256
ROWS_DIL = 512
CUM_CHUNK = 256


def _params(*sem):
    return pltpu.CompilerParams(dimension_semantics=sem, vmem_limit_bytes=VMEM_LIMIT_BYTES)


def _rms(x, gain):
    return (x * lax.rsqrt(jnp.mean(x * x, axis=-1, keepdims=True) + EPS)) * gain


def _dot(a, b):
    return jnp.dot(a, b, preferred_element_type=F32)


def _dot_nt(a, b):
    return lax.dot_general(a, b, (((1,), (1,)), ((), ())), preferred_element_type=F32)


def _rope128(y, cos, sin):
    return y * cos + pltpu.roll(y, 64, 1) * sin


def _in_proj_kernel(x_ref, g_ref, w_ref, wt_ref, o_ref, ot_ref, h_ref):
    @pl.when(pl.program_id(1) == 0)
    def _():
        hb = _rms(x_ref[...], g_ref[...]).astype(BF16)
        h_ref[...] = hb
        ot_ref[...] = _dot(hb, wt_ref[...])

    o_ref[...] = _dot(h_ref[...], w_ref[...]).astype(o_ref.dtype)


def _in_proj(x, gain, w_main, w_tail):
    m = x.shape[0]
    return pl.pallas_call(
        _in_proj_kernel,
        grid=(m // TM_IN, MAIN_WIDTH // TN_IN),
        in_specs=[
            pl.BlockSpec((TM_IN, D_MODEL), lambda i, j: (i, 0)),
            pl.BlockSpec((1, D_MODEL), lambda i, j: (0, 0)),
            pl.BlockSpec((D_MODEL, TN_IN), lambda i, j: (0, j)),
            pl.BlockSpec((D_MODEL, TAIL_WIDTH), lambda i, j: (0, 0)),
        ],
        out_specs=[
            pl.BlockSpec((TM_IN, TN_IN), lambda i, j: (i, j)),
            pl.BlockSpec((TM_IN, TAIL_WIDTH), lambda i, j: (i, 0)),
        ],
        out_shape=[
            jax.ShapeDtypeStruct((m, MAIN_WIDTH), BF16),
            jax.ShapeDtypeStruct((m, TAIL_WIDTH), F32),
        ],
        scratch_shapes=[pltpu.VMEM((TM_IN, D_MODEL), BF16)],
        compiler_params=_params("parallel", "arbitrary"),
        name="in_proj",
    )(x, gain, w_main, w_tail)


def _mla_prep_kernel(ql_ref, kvl_ref, t_ref, gq_ref, gkv_ref, wq_ref, wkv_ref, cos_ref, sin_ref,
                     q_ref, kv_ref, kpe_ref):
    cos = cos_ref[...]
    sin = sin_ref[...]
    hq = _rms(ql_ref[...].astype(F32), gq_ref[...]).astype(BF16)
    q = _dot(hq, wq_ref[...])
    for h in range(N_HEADS):
        lo = h * MLA_HEAD_PAD
        q_ref[:, lo:lo + HEAD_DIM] = q[:, lo:lo + HEAD_DIM].astype(BF16)
        y = q[:, lo + HEAD_DIM:lo + MLA_HEAD_PAD]
        q_ref[:, lo + HEAD_DIM:lo + MLA_HEAD_PAD] = _rope128(y, cos, sin).astype(BF16)
    hkv = _rms(kvl_ref[...].astype(F32), gkv_ref[...]).astype(BF16)
    kv_ref[...] = _dot(hkv, wkv_ref[...]).astype(BF16)
    kpe_ref[...] = _rope128(t_ref[...], cos, sin).astype(BF16)


def _mla_prep(proj, tail, gq, gkv, wq, wkv, cos_t, sin_t, seq):
    m = proj.shape[0]
    nseq = seq // TM_MLA
    return pl.pallas_call(
        _mla_prep_kernel,
        grid=(m // TM_MLA,),
        in_specs=[
            pl.BlockSpec((TM_MLA, Q_LORA), lambda i: (i, 0)),
            pl.BlockSpec((TM_MLA, KV_LORA), lambda i: (i, 1)),
            pl.BlockSpec((TM_MLA, HEAD_DIM), lambda i: (i, 0)),
            pl.BlockSpec((1, Q_LORA), lambda i: (0, 0)),
            pl.BlockSpec((1, KV_LORA), lambda i: (0, 0)),
            pl.BlockSpec((Q_LORA, N_HEADS * MLA_HEAD_PAD), lambda i: (0, 0)),
            pl.BlockSpec((KV_LORA, 2 * GROUP_WIDTH), lambda i: (0, 0)),
            pl.BlockSpec((TM_MLA, HEAD_DIM), lambda i: (i % nseq, 0)),
            pl.BlockSpec((TM_MLA, HEAD_DIM), lambda i: (i % nseq, 0)),
        ],
        out_specs=[
            pl.BlockSpec((TM_MLA, N_HEADS * MLA_HEAD_PAD), lambda i: (i, 0)),
            pl.BlockSpec((TM_MLA, 2 * GROUP_WIDTH), lambda i: (i, 0)),
            pl.BlockSpec((TM_MLA, HEAD_DIM), lambda i: (i, 0)),
        ],
        out_shape=[
            jax.ShapeDtypeStruct((m, N_HEADS * MLA_HEAD_PAD), BF16),
            jax.ShapeDtypeStruct((m, 2 * GROUP_WIDTH), BF16),
            jax.ShapeDtypeStruct((m, HEAD_DIM), BF16),
        ],
        compiler_params=_params("parallel"),
        name="mla_prep",
    )(proj, proj, tail, gq, gkv, wq, wkv, cos_t, sin_t)


def _split3(x):
    hi = x.astype(BF16)
    r1 = x - hi.astype(F32)
    mid = r1.astype(BF16)
    lo = (r1 - mid.astype(F32)).astype(BF16)
    return hi, mid, lo


def _fox_prep_kernel(t_ref, b_ref, col_ref, row_ref, *, seq):
    r = lax.broadcasted_iota(jnp.int32, (CUM_CHUNK, CUM_CHUNK), 0)
    c = lax.broadcasted_iota(jnp.int32, (CUM_CHUNK, CUM_CHUNK), 1)
    lower = jnp.where(c <= r, 1.0, 0.0).astype(BF16)
    carry = jnp.zeros((1, HEAD_DIM), F32)
    for ci in range(seq // CUM_CHUNK):
        rows = slice(ci * CUM_CHUNK, (ci + 1) * CUM_CHUNK)
        x = t_ref[rows, :] + b_ref[...]
        log_f = jnp.minimum(x, 0.0) - jnp.log(1.0 + jnp.exp(-jnp.abs(x)))
        hi, mid, lo = _split3(log_f)
        cum = (_dot(lower, hi) + _dot(lower, mid)) + _dot(lower, lo) + carry
        col_ref[rows, :] = cum
        cum_t = cum.T
        for h in range(N_HEADS):
            row_ref[0, h, :, rows] = cum_t[h:h + 1, :]
        carry = cum[CUM_CHUNK - 1:CUM_CHUNK, :]


def _fox_prep(tail, bias_row, batch, seq):
    m = tail.shape[0]
    return pl.pallas_call(
        functools.partial(_fox_prep_kernel, seq=seq),
        grid=(batch,),
        in_specs=[
            pl.BlockSpec((seq, HEAD_DIM), lambda b: (b, 1)),
            pl.BlockSpec((1, HEAD_DIM), lambda b: (0, 0)),
        ],
        out_specs=[
            pl.BlockSpec((seq, HEAD_DIM), lambda b: (b, 0)),
            pl.BlockSpec((1, N_HEADS, 1, seq), lambda b: (b, 0, 0, 0)),
        ],
        out_shape=[
            jax.ShapeDtypeStruct((m, HEAD_DIM), F32),
            jax.ShapeDtypeStruct((batch, N_HEADS, 1, seq), F32),
        ],
        compiler_params=_params("parallel"),
        name="fox_prep",
    )(tail, bias_row)


def _flash_kernel(*refs, scale, mla):
    if mla:
        q_ref, k_ref, kpe_ref, v_ref, o_ref, m_s, acc_s = refs
    else:
        q_ref, k_ref, v_ref, ccol_ref, crow_ref, o_ref, m_s, acc_s, ct_s = refs
    fox = not mla
    t = TK_ATT
    n_sub = TQ_ATT // t
    reps = t // HEAD_DIM
    head = pl.program_id(1)
    qi = pl.program_id(2)
    ones = jnp.ones((t, HEAD_DIM), BF16)
    m_s[...] = jnp.full(m_s.shape, NEG_INF, F32)
    acc_s[...] = jnp.zeros(acc_s.shape, F32)
    causal = (lax.broadcasted_iota(jnp.int32, (t, t), 1) <= lax.broadcasted_iota(jnp.int32, (t, t), 0))
    if fox:
        lane = lax.broadcasted_iota(jnp.int32, (TQ_ATT, HEAD_DIM), 1)
        c_t = jnp.sum(jnp.where(lane == head, ccol_ref[...], 0.0), axis=1, keepdims=True)
        ct_s[...] = jnp.broadcast_to(c_t * LOG2E, (TQ_ATT, HEAD_DIM))

    def chunk(start, subs):
        k = k_ref[pl.ds(start, t), :]
        if mla:
            k = jnp.concatenate([k, kpe_ref[pl.ds(start, t), :]], axis=1)
        v_aug = jnp.concatenate([v_ref[pl.ds(start, t), :], ones], axis=1)
        rows = [slice(sub * t, (sub + 1) * t) for sub, _ in subs]
        scores = [_dot_nt(q_ref[r, :], k) for r in rows]
        if fox:
            c_s = crow_ref[0, 0, :, pl.ds(start, t)] * LOG2E
        probs, alphas = [], []
        for r, s, (_, diagonal) in zip(rows, scores, subs):
            if fox:
                s = s * (scale * LOG2E) - c_s
            if diagonal:
                s = jnp.where(causal, s, NEG_INF)
            m_prev = m_s[r, :]
            top = jnp.max(s, axis=-1, keepdims=True)
            if fox:
                top = top + ct_s[r, :]
            m_new = jnp.maximum(m_prev, top)
            m_s[r, :] = m_new
            if fox:
                alphas.append(jnp.exp2(m_prev - m_new))
                shift = m_new - ct_s[r, :]
                probs.append(jnp.exp2(s - pltpu.repeat(shift, reps, axis=1)).astype(BF16))
            else:
                alphas.append(jnp.exp2((m_prev - m_new) * (scale * LOG2E)))
                probs.append(jnp.exp2((s - pltpu.repeat(m_new, reps, axis=1)) * (scale * LOG2E)).astype(BF16))
        for r, p, alpha in zip(rows, probs, alphas):
            acc_s[r, :] = pltpu.repeat(alpha, 2, axis=1) * acc_s[r, :] + _dot(p, v_aug)

    def body(c, carry):
        chunk(pl.multiple_of(c * t, t), tuple((sub, False) for sub in range(n_sub)))
        return carry

    lax.fori_loop(0, qi * n_sub, body, 0)
    base = qi * TQ_ATT
    for j in range(n_sub):
        chunk(pl.multiple_of(base + j * t, t), tuple((sub, sub == j) for sub in range(j, n_sub)))
    acc = acc_s[...]
    o_ref[...] = (acc[:, :HEAD_DIM] / acc[:, HEAD_DIM:]).astype(o_ref.dtype)


def _flash_scratch(fox):
    stats = [pltpu.VMEM((TQ_ATT, HEAD_DIM), F32), pltpu.VMEM((TQ_ATT, 2 * HEAD_DIM), F32)]
    return stats + ([pltpu.VMEM((TQ_ATT, HEAD_DIM), F32)] if fox else [])


def _mla_attention(q_a, kv_a, kpe, batch, seq):
    m = q_a.shape[0]
    nq = seq // TQ_ATT
    return pl.pallas_call(
        functools.partial(_flash_kernel, scale=MLA_QK_DIM ** -0.5, mla=True),
        grid=(batch, N_HEADS, nq),
        in_specs=[
            pl.BlockSpec((TQ_ATT, MLA_HEAD_PAD), lambda b, h, i: (b * nq + i, h)),
            pl.BlockSpec((seq, HEAD_DIM), lambda b, h, i: (b, h)),
            pl.BlockSpec((seq, HEAD_DIM), lambda b, h, i: (b, 0)),
            pl.BlockSpec((seq, HEAD_DIM), lambda b, h, i: (b, N_HEADS + h)),
        ],
        out_specs=pl.BlockSpec((TQ_ATT, HEAD_DIM), lambda b, h, i: (b * nq + i, h)),
        out_shape=jax.ShapeDtypeStruct((m, GROUP_WIDTH), BF16),
        scratch_shapes=_flash_scratch(False),
        compiler_params=_params("parallel", "parallel", "arbitrary"),
        name="mla_attention",
    )(q_a, kv_a, kpe, kv_a)


def _fox_attention(proj, ccol, crow, batch, seq):
    m = proj.shape[0]
    nq = seq // TQ_ATT
    return pl.pallas_call(
        functools.partial(_flash_kernel, scale=HEAD_DIM ** -0.5, mla=False),
        grid=(batch, N_HEADS, nq),
        in_specs=[
            pl.BlockSpec((TQ_ATT, HEAD_DIM), lambda b, h, i: (b * nq + i, COL_QC + h)),
            pl.BlockSpec((seq, HEAD_DIM), lambda b, h, i: (b, COL_KC + h)),
            pl.BlockSpec((seq, HEAD_DIM), lambda b, h, i: (b, COL_VC + h)),
            pl.BlockSpec((TQ_ATT, HEAD_DIM), lambda b, h, i: (b * nq + i, 0)),
            pl.BlockSpec((1, 1, 1, seq), lambda b, h, i: (b, h, 0, 0)),
        ],
        out_specs=pl.BlockSpec((TQ_ATT, HEAD_DIM), lambda b, h, i: (b * nq + i, h)),
        out_shape=jax.ShapeDtypeStruct((m, GROUP_WIDTH), BF16),
        scratch_shapes=_flash_scratch(True),
        compiler_params=_params("parallel", "parallel", "arbitrary"),
        name="fox_attention",
    )(proj, proj, proj, ccol, crow)


def _sb_kernel(q_ref, k_ref, v_ref, o_ref, r_s, acc_s, *, scale):
    t = T_SB
    n_sub = TQ_SB // t
    reps = t // HEAD_DIM
    qi = pl.program_id(2)
    row = lax.broadcasted_iota(jnp.int32, (t, t), 0)
    col = lax.broadcasted_iota(jnp.int32, (t, t), 1)
    later = jnp.where(row > col, 1.0, 0.0).astype(BF16)
    past = col < row
    r_s[...] = jnp.zeros(r_s.shape, F32)
    acc_s[...] = jnp.zeros(acc_s.shape, F32)

    def chunk(start, subs):
        k = k_ref[pl.ds(start, t), :]
        v = v_ref[pl.ds(start, t), :]
        rows = [slice(sub * t, (sub + 1) * t) for sub, _ in subs]
        logits = [_dot_nt(q_ref[r, :], k) * (scale * LOG2E) for r in rows]
        log_betas, log_keeps, his, los = [], [], [], []
        for z, (_, diagonal) in zip(logits, subs):
            soft = jnp.log2(1.0 + jnp.exp2(-jnp.abs(z)))
            log_beta = jnp.minimum(z, 0.0) - soft
            log_keep = log_beta - z
            if diagonal:
                log_keep = jnp.where(past, log_keep, 0.0)
            hi = log_keep.astype(BF16)
            log_betas.append(log_beta)
            log_keeps.append(log_keep)
            his.append(hi)
            los.append((log_keep - hi.astype(F32)).astype(BF16))
        betweens = [_dot(hi, later) + _dot(lo, later) for hi, lo in zip(his, los)]
        weights = []
        for r, log_beta, between, (_, diagonal) in zip(rows, log_betas, betweens, subs):
            a = jnp.exp2(log_beta + (between + pltpu.repeat(r_s[r, :], reps, axis=1)))
            if diagonal:
                a = jnp.where(past, a, 0.0)
            weights.append(a.astype(BF16))
        for r, a, log_keep in zip(rows, weights, log_keeps):
            acc_s[r, :] += _dot(a, v)
            r_s[r, :] += jnp.sum(log_keep, axis=-1, keepdims=True)

    base = qi * TQ_SB
    for j in reversed(range(n_sub)):
        chunk(pl.multiple_of(base + j * t, t), tuple((sub, sub == j) for sub in range(j, n_sub)))

    def body(c, carry):
        chunk(pl.multiple_of(base - (c + 1) * t, t), tuple((sub, False) for sub in range(n_sub)))
        return carry

    lax.fori_loop(0, qi * n_sub, body, 0)
    o_ref[...] = acc_s[...].astype(o_ref.dtype)


def _sb_attention(proj, batch, seq):
    m = proj.shape[0]
    nq = seq // TQ_SB
    return pl.pallas_call(
        functools.partial(_sb_kernel, scale=HEAD_DIM ** -0.5),
        grid=(batch, N_HEADS, nq),
        in_specs=[
            pl.BlockSpec((TQ_SB, HEAD_DIM), lambda b, h, i: (b * nq + i, COL_QD + h)),
            pl.BlockSpec((seq, HEAD_DIM), lambda b, h, i: (b, COL_KD + h)),
            pl.BlockSpec((seq, HEAD_DIM), lambda b, h, i: (b, COL_VD + h)),
        ],
        out_specs=pl.BlockSpec((TQ_SB, HEAD_DIM), lambda b, h, i: (b * nq + i, h)),
        out_shape=jax.ShapeDtypeStruct((m, GROUP_WIDTH), BF16),
        scratch_shapes=[pltpu.VMEM((TQ_SB, HEAD_DIM), F32), pltpu.VMEM((TQ_SB, HEAD_DIM), F32)],
        compiler_params=_params("parallel", "parallel", "arbitrary"),
        name="sb_attention",
    )(proj, proj, proj)


def _dilated_kernel(q_ref, k_ref, v_ref, cos_ref, sin_ref, o_ref, q_s, k_s, v_s, og_s, lg_s, *, seq, scale):
    for c in range(seq // ROWS_DIL):
        rows = slice(c * ROWS_DIL, (c + 1) * ROWS_DIL)
        cos = cos_ref[rows, :]
        sin = sin_ref[rows, :]
        q_s[rows, :] = _rope128(q_ref[rows, :].astype(F32), cos, sin)
        k_s[rows, :] = _rope128(k_ref[rows, :].astype(F32), cos, sin)
        v_s[rows, :] = v_ref[rows, :].astype(F32)

    ones = jnp.ones((2 * BAND, HEAD_DIM), BF16)
    row1 = lax.broadcasted_iota(jnp.int32, (BAND, BAND), 0)
    col1 = lax.broadcasted_iota(jnp.int32, (BAND, BAND), 1)
    valid1 = col1 <= row1
    row2 = lax.broadcasted_iota(jnp.int32, (BAND, 2 * BAND), 0)
    col2 = lax.broadcasted_iota(jnp.int32, (BAND, 2 * BAND), 1)
    dist = BAND + row2 - col2
    valid2 = (dist >= 0) & (dist <= BAND)

    def attend(g, d, q_start, k_start, n_keys, valid):
        def strided(start, n):
            return pl.ds(start, n, stride=d) if d > 1 else pl.ds(start, n)
        qb = q_s[strided(q_start, BAND), :].astype(BF16)
        kb = k_s[strided(k_start, n_keys), :].astype(BF16)
        vb = v_s[strided(k_start, n_keys), :].astype(BF16)
        s = jnp.where(valid, _dot_nt(qb, kb), NEG_INF)
        m = jnp.max(s, axis=-1, keepdims=True)
        p = jnp.exp2((s - m) * (scale * LOG2E))
        pv = _dot(p.astype(BF16), jnp.concatenate([vb, ones[:n_keys]], axis=1))
        den = pv[:, HEAD_DIM:]
        og_s[g, strided(q_start, BAND), :] = pv[:, :HEAD_DIM] / den
        lg_s[g, strided(q_start, BAND), :] = m * scale + jnp.log(den)

    for g, d in enumerate(DILATIONS):
        nb = seq // d // BAND

        def residue(r, carry, g=g, d=d, nb=nb):
            attend(g, d, r, r, BAND, valid1)

            def block(i, c2):
                attend(g, d, r + i * (BAND * d), r + (i - 1) * (BAND * d), 2 * BAND, valid2)
                return c2

            lax.fori_loop(1, nb, block, 0, unroll=4 if nb > 2 else 1)
            return carry

        lax.fori_loop(0, d, residue, 0, unroll=4 if (nb <= 2 and d > 1) else 1)

    for c in range(seq // ROWS_DIL):
        rows = slice(c * ROWS_DIL, (c + 1) * ROWS_DIL)
        l1, l2, l3 = lg_s[0, rows, :], lg_s[1, rows, :], lg_s[2, rows, :]
        top = jnp.maximum(jnp.maximum(l1, l2), l3)
        e1, e2, e3 = jnp.exp(l1 - top), jnp.exp(l2 - top), jnp.exp(l3 - top)
        num = (e1 * og_s[0, rows, :] + e2 * og_s[1, rows, :]) + e3 * og_s[2, rows, :]
        o_ref[rows, :] = (num / ((e1 + e2) + e3)).astype(o_ref.dtype)


def _dilated_attention(proj, cos_b, sin_b, batch, seq):
    m = proj.shape[0]
    blk = (seq, HEAD_DIM)
    n_groups = len(DILATIONS)
    return pl.pallas_call(
        functools.partial(_dilated_kernel, seq=seq, scale=HEAD_DIM ** -0.5),
        grid=(batch, N_HEADS),
        in_specs=[
            pl.BlockSpec(blk, lambda b, h: (b, COL_QB + h)),
            pl.BlockSpec(blk, lambda b, h: (b, COL_KB + h)),
            pl.BlockSpec(blk, lambda b, h: (b, COL_VB + h)),
            pl.BlockSpec(blk, lambda b, h: (0, 0)),
            pl.BlockSpec(blk, lambda b, h: (0, 0)),
        ],
        out_specs=pl.BlockSpec(blk, lambda b, h: (b, h)),
        out_shape=jax.ShapeDtypeStruct((m, GROUP_WIDTH), BF16),
        scratch_shapes=[pltpu.VMEM(blk, F32), pltpu.VMEM(blk, F32), pltpu.VMEM(blk, F32),
                        pltpu.VMEM((n_groups,) + blk, F32), pltpu.VMEM((n_groups,) + blk, F32)],
        compiler_params=_params("parallel", "parallel"),
        name="dilated_attention",
    )(proj, proj, proj, cos_b, sin_b)


def _out_proj_kernel(oa_ref, ob_ref, oc_ref, od_ref, gn_ref, w_ref, x_ref, o_ref):
    for sub in range(TM_OUT // ROWS_OUT):
        rows = slice(sub * ROWS_OUT, (sub + 1) * ROWS_OUT)
        parts = []
        for group, ref in enumerate((oa_ref, ob_ref, oc_ref, od_ref)):
            cols = slice(group * GROUP_WIDTH, (group + 1) * GROUP_WIDTH)
            parts.append(_rms(ref[rows, :].astype(F32), gn_ref[:, cols]).astype(BF16))
        g = jnp.concatenate(parts, axis=1)
        o_ref[rows, :] = x_ref[rows, :] + _dot(g, w_ref[...])


def _out_proj(oa, ob, oc, od, gn, w_out, x):
    m = x.shape[0]
    grp = pl.BlockSpec((TM_OUT, GROUP_WIDTH), lambda i: (i, 0))
    return pl.pallas_call(
        _out_proj_kernel,
        grid=(m // TM_OUT,),
        in_specs=[
            grp, grp, grp, grp,
            pl.BlockSpec((1, D_MODEL), lambda i: (0, 0)),
            pl.BlockSpec((D_MODEL, D_MODEL), lambda i: (0, 0)),
            pl.BlockSpec((TM_OUT, D_MODEL), lambda i: (i, 0)),
        ],
        out_specs=pl.BlockSpec((TM_OUT, D_MODEL), lambda i: (i, 0)),
        out_shape=jax.ShapeDtypeStruct((m, D_MODEL), F32),
        compiler_params=_params("parallel"),
        name="out_proj",
    )(oa, ob, oc, od, gn, w_out, x)


def _ffn_kernel(x_ref, g_ref, wg_ref, wu_ref, wd_ref, fg_ref, o_ref, h_s, *, final):
    f = pl.program_id(1)

    @pl.when(f == 0)
    def _():
        x = x_ref[...]
        h_s[...] = _rms(x, g_ref[...]).astype(BF16)
        o_ref[...] = x

    h = h_s[...]
    gate = _dot(h, wg_ref[...])
    up = _dot(h, wu_ref[...])
    act = (gate / (1.0 + jnp.exp(-gate))) * up
    o_ref[...] += _dot(act.astype(BF16), wd_ref[...])

    if final:
        @pl.when(f == pl.num_programs(1) - 1)
        def _():
            o_ref[...] = _rms(o_ref[...], fg_ref[...])


def _ffn(x, gain, w_gate, w_up, w_down, final_gain, final):
    m = x.shape[0]
    return pl.pallas_call(
        functools.partial(_ffn_kernel, final=final),
        grid=(m // TM_FFN, FFN_HIDDEN // TF_FFN),
        in_specs=[
            pl.BlockSpec((TM_FFN, D_MODEL), lambda i, f: (i, 0)),
            pl.BlockSpec((1, D_MODEL), lambda i, f: (0, 0)),
            pl.BlockSpec((D_MODEL, TF_FFN), lambda i, f: (0, f)),
            pl.BlockSpec((D_MODEL, TF_FFN), lambda i, f: (0, f)),
            pl.BlockSpec((TF_FFN, D_MODEL), lambda i, f: (f, 0)),
            pl.BlockSpec((1, D_MODEL), lambda i, f: (0, 0)),
        ],
        out_specs=pl.BlockSpec((TM_FFN, D_MODEL), lambda i, f: (i, 0)),
        out_shape=jax.ShapeDtypeStruct((m, D_MODEL), F32),
        scratch_shapes=[pltpu.VMEM((TM_FFN, D_MODEL), BF16)],
        compiler_params=_params("parallel", "arbitrary"),
        name="ffn_final" if final else "ffn",
    )(x, gain, w_gate, w_up, w_down, final_gain)


def _rot_cols(w):
    half = w.shape[-1] // 2
    return jnp.concatenate([-w[..., half:], w[..., :half]], axis=-1)


def _prep_weights(w_in, w_uq, w_ukv, fox_forget_bias):
    depth = w_in.shape[0]
    k_rope = w_in[..., 1024:1088]
    w_main = jnp.concatenate([w_in[..., 0:1024], w_in[..., 1088:4160], w_in[..., 4164:5700]], axis=-1)
    w_tail = jnp.concatenate([k_rope, _rot_cols(k_rope), w_in[..., 4160:4164],
                              jnp.zeros((depth, D_MODEL, HEAD_DIM - N_HEADS), w_in.dtype)], axis=-1)
    uq = w_uq.reshape(depth, Q_LORA, N_HEADS, MLA_QK_DIM)
    pe = uq[..., QK_NOPE:]
    uq = jnp.concatenate([uq[..., :QK_NOPE], pe, _rot_cols(pe)], axis=-1)
    uq = uq.reshape(depth, Q_LORA, N_HEADS * MLA_HEAD_PAD)
    ukv = w_ukv.reshape(depth, KV_LORA, N_HEADS, 2, HEAD_DIM).transpose(0, 1, 3, 2, 4)
    ukv = ukv.reshape(depth, KV_LORA, 2 * GROUP_WIDTH)
    bias_rows = jnp.pad(fox_forget_bias, ((0, 0), (0, HEAD_DIM - N_HEADS)))[:, None, :]
    return w_main.astype(BF16), w_tail.astype(BF16), uq.astype(BF16), ukv.astype(BF16), bias_rows


def _rope_tables(seq):
    pos = jnp.arange(seq, dtype=F32)[:, None]

    def angles(dim):
        inv_freq = ROPE_THETA ** (-jnp.arange(0, dim, 2, dtype=F32) / dim)
        return pos * inv_freq[None, :]

    ang_b = angles(HEAD_DIM)
    cos_b = jnp.concatenate([jnp.cos(ang_b)] * 2, axis=1)
    sin_b = jnp.concatenate([-jnp.sin(ang_b), jnp.sin(ang_b)], axis=1)
    ang_a = angles(QK_ROPE)
    zeros = jnp.zeros((seq, HEAD_DIM - QK_ROPE), F32)
    cos_a = jnp.concatenate([jnp.cos(ang_a)] * 2 + [zeros], axis=1)
    sin_a = jnp.concatenate([jnp.sin(ang_a)] * 2 + [zeros], axis=1)
    return cos_a, sin_a, cos_b, sin_b


def kernel(x, attn_norm, w_in, mla_q_norm, w_uq, mla_kv_norm, w_ukv, fox_forget_bias, group_norm,
           w_out, ffn_norm, w_gate, w_up, w_down, final_norm):
    batch, seq, _ = x.shape
    m = batch * seq
    assert seq % TQ_ATT == 0 and seq % (max(DILATIONS) * BAND) == 0 and m % TM_IN == 0
    w_main, w_tail, uq, ukv, bias_rows = _prep_weights(w_in, w_uq, w_ukv, fox_forget_bias)
    w_out_b, w_gate_b, w_up_b, w_down_b = (w.astype(BF16) for w in (w_out, w_gate, w_up, w_down))
    cos_a, sin_a, cos_b, sin_b = _rope_tables(seq)
    row = lambda v: v.reshape(1, -1)

    xf = x.reshape(m, D_MODEL)
    for l in range(DEPTH):
        proj, tail = _in_proj(xf, row(attn_norm[l]), w_main[l], w_tail[l])
        q_a, kv_a, kpe = _mla_prep(proj, tail, row(mla_q_norm[l]), row(mla_kv_norm[l]), uq[l], ukv[l],
                                   cos_a, sin_a, seq)
        ccol, crow = _fox_prep(tail, bias_rows[l], batch, seq)
        out_a = _mla_attention(q_a, kv_a, kpe, batch, seq)
        out_b = _dilated_attention(proj, cos_b, sin_b, batch, seq)
        out_c = _fox_attention(proj, ccol, crow, batch, seq)
        out_d = _sb_attention(proj, batch, seq)
        xf = _out_proj(out_a, out_b, out_c, out_d, row(group_norm[l]), w_out_b[l], xf)
        xf = _ffn(xf, row(ffn_norm[l]), w_gate_b[l], w_up_b[l], w_down_b[l], row(final_norm),
                  final=(l == DEPTH - 1))
    return xf.reshape(batch, seq, D_MODEL)
```

```python
import functools
import math

import jax
import jax.numpy as jnp
import numpy as np
from jax import lax
from jax.experimental import pallas as pl
from jax.experimental.pallas import tpu as pltpu

F32 = jnp.float32
BF16 = jnp.bfloat16

D_MODEL = 2048
DEPTH = 4
N_HEADS = 4
HEAD_DIM = 128
GROUP_WIDTH = N_HEADS * HEAD_DIM
Q_LORA = 512
KV_LORA = 512
QK_NOPE = 128
QK_ROPE = 64
MLA_QK_DIM = QK_NOPE + QK_ROPE
MLA_HEAD_PAD = 2 * HEAD_DIM
DILATED_PAIRS = ((128, 1), (512, 4), (2048, 16))
BAND = 128
assert all(window // dilation == BAND for window, dilation in DILATED_PAIRS)
DILATIONS = tuple(dilation for _, dilation in DILATED_PAIRS)
ROPE_THETA = 10000.0
FFN_HIDDEN = 5632
EPS = 1e-6
NEG_INF = -1e30
LOG2E = math.log2(math.e)

IN_SPLITS = ((Q_LORA, KV_LORA, QK_ROPE) + (GROUP_WIDTH,) * 3 + (GROUP_WIDTH,) * 3 + (N_HEADS,)
             + (GROUP_WIDTH,) * 3)
IN_OFFSETS = tuple(int(v) for v in np.cumsum((0,) + IN_SPLITS))

MAIN_WIDTH = Q_LORA + KV_LORA + 9 * GROUP_WIDTH
TAIL_WIDTH = 2 * HEAD_DIM
COL_QB, COL_KB, COL_VB = 8, 12, 16
COL_QC, COL_KC, COL_VC = 20, 24, 28
COL_QD, COL_KD, COL_VD = 32, 36, 40

VMEM_LIMIT_BYTES = 56 * 1024 * 1024
TM_IN, TN_IN = 1024, 512
TM_MLA = 1024
TM_OUT = 512
ROWS_OUT = 128
TM_FFN, TF_FFN = 1024, 512
TQ_ATT, TK_ATT = 2048, 512
TQ_SB, T_SB = 1024, 256
SB_DEAD_LOG2 = -150.0
ROWS_DIL = 512
CUM_CHUNK = 256


def _params(*sem):
    return pltpu.CompilerParams(dimension_semantics=sem, vmem_limit_bytes=VMEM_LIMIT_BYTES)


def _rms(x, gain):
    return (x * lax.rsqrt(jnp.mean(x * x, axis=-1, keepdims=True) + EPS)) * gain


def _dot(a, b):
    return jnp.dot(a, b, preferred_element_type=F32)


def _dot_nt(a, b):
    return lax.dot_general(a, b, (((1,), (1,)), ((), ())), preferred_element_type=F32)


def _lane_tile(x, n):
    return jnp.concatenate([x] * n, axis=1)


def _rope128(y, cos, sin):
    return y * cos + pltpu.roll(y, 64, 1) * sin


def _in_proj_kernel(x_ref, g_ref, w_ref, wt_ref, o_ref, ot_ref, h_ref):
    @pl.when(pl.program_id(1) == 0)
    def _():
        hb = _rms(x_ref[...], g_ref[...]).astype(BF16)
        h_ref[...] = hb
        ot_ref[...] = _dot(hb, wt_ref[...])

    o_ref[...] = _dot(h_ref[...], w_ref[...]).astype(o_ref.dtype)


def _in_proj(x, gain, w_main, w_tail):
    m = x.shape[0]
    return pl.pallas_call(
        _in_proj_kernel,
        grid=(m // TM_IN, MAIN_WIDTH // TN_IN),
        in_specs=[
            pl.BlockSpec((TM_IN, D_MODEL), lambda i, j: (i, 0)),
            pl.BlockSpec((1, D_MODEL), lambda i, j: (0, 0)),
            pl.BlockSpec((D_MODEL, TN_IN), lambda i, j: (0, j)),
            pl.BlockSpec((D_MODEL, TAIL_WIDTH), lambda i, j: (0, 0)),
        ],
        out_specs=[
            pl.BlockSpec((TM_IN, TN_IN), lambda i, j: (i, j)),
            pl.BlockSpec((TM_IN, TAIL_WIDTH), lambda i, j: (i, 0)),
        ],
        out_shape=[
            jax.ShapeDtypeStruct((m, MAIN_WIDTH), BF16),
            jax.ShapeDtypeStruct((m, TAIL_WIDTH), F32),
        ],
        scratch_shapes=[pltpu.VMEM((TM_IN, D_MODEL), BF16)],
        compiler_params=_params("parallel", "arbitrary"),
        name="in_proj",
    )(x, gain, w_main, w_tail)


def _mla_prep_kernel(ql_ref, kvl_ref, t_ref, gq_ref, gkv_ref, wq_ref, wkv_ref, cos_ref, sin_ref,
                     q_ref, kv_ref, kpe_ref):
    cos = cos_ref[...]
    sin = sin_ref[...]
    hq = _rms(ql_ref[...].astype(F32), gq_ref[...]).astype(BF16)
    q = _dot(hq, wq_ref[...])
    for h in range(N_HEADS):
        lo = h * MLA_HEAD_PAD
        q_ref[:, lo:lo + HEAD_DIM] = q[:, lo:lo + HEAD_DIM].astype(BF16)
        y = q[:, lo + HEAD_DIM:lo + MLA_HEAD_PAD]
        q_ref[:, lo + HEAD_DIM:lo + MLA_HEAD_PAD] = _rope128(y, cos, sin).astype(BF16)
    hkv = _rms(kvl_ref[...].astype(F32), gkv_ref[...]).astype(BF16)
    kv_ref[...] = _dot(hkv, wkv_ref[...]).astype(BF16)
    kpe_ref[...] = _rope128(t_ref[...], cos, sin).astype(BF16)


def _mla_prep(proj, tail, gq, gkv, wq, wkv, cos_t, sin_t, seq):
    m = proj.shape[0]
    nseq = seq // TM_MLA
    return pl.pallas_call(
        _mla_prep_kernel,
        grid=(m // TM_MLA,),
        in_specs=[
            pl.BlockSpec((TM_MLA, Q_LORA), lambda i: (i, 0)),
            pl.BlockSpec((TM_MLA, KV_LORA), lambda i: (i, 1)),
            pl.BlockSpec((TM_MLA, HEAD_DIM), lambda i: (i, 0)),
            pl.BlockSpec((1, Q_LORA), lambda i: (0, 0)),
            pl.BlockSpec((1, KV_LORA), lambda i: (0, 0)),
            pl.BlockSpec((Q_LORA, N_HEADS * MLA_HEAD_PAD), lambda i: (0, 0)),
            pl.BlockSpec((KV_LORA, 2 * GROUP_WIDTH), lambda i: (0, 0)),
            pl.BlockSpec((TM_MLA, HEAD_DIM), lambda i: (i % nseq, 0)),
            pl.BlockSpec((TM_MLA, HEAD_DIM), lambda i: (i % nseq, 0)),
        ],
        out_specs=[
            pl.BlockSpec((TM_MLA, N_HEADS * MLA_HEAD_PAD), lambda i: (i, 0)),
            pl.BlockSpec((TM_MLA, 2 * GROUP_WIDTH), lambda i: (i, 0)),
            pl.BlockSpec((TM_MLA, HEAD_DIM), lambda i: (i, 0)),
        ],
        out_shape=[
            jax.ShapeDtypeStruct((m, N_HEADS * MLA_HEAD_PAD), BF16),
            jax.ShapeDtypeStruct((m, 2 * GROUP_WIDTH), BF16),
            jax.ShapeDtypeStruct((m, HEAD_DIM), BF16),
        ],
        compiler_params=_params("parallel"),
        name="mla_prep",
    )(proj, proj, tail, gq, gkv, wq, wkv, cos_t, sin_t)


def _split3(x):
    hi = x.astype(BF16)
    r1 = x - hi.astype(F32)
    mid = r1.astype(BF16)
    lo = (r1 - mid.astype(F32)).astype(BF16)
    return hi, mid, lo


def _fox_prep_kernel(t_ref, b_ref, col_ref, row_ref, *, seq):
    r = lax.broadcasted_iota(jnp.int32, (CUM_CHUNK, CUM_CHUNK), 0)
    c = lax.broadcasted_iota(jnp.int32, (CUM_CHUNK, CUM_CHUNK), 1)
    lower = jnp.where(c <= r, 1.0, 0.0).astype(BF16)
    carry = jnp.zeros((1, HEAD_DIM), F32)
    for ci in range(seq // CUM_CHUNK):
        rows = slice(ci * CUM_CHUNK, (ci + 1) * CUM_CHUNK)
        x = t_ref[rows, :] + b_ref[...]
        log_f = jnp.minimum(x, 0.0) - jnp.log(1.0 + jnp.exp(-jnp.abs(x)))
        hi, mid, lo = _split3(log_f)
        cum = (_dot(lower, hi) + _dot(lower, mid)) + _dot(lower, lo) + carry
        col_ref[rows, :] = cum
        cum_t = cum.T
        for h in range(N_HEADS):
            row_ref[0, h, :, rows] = cum_t[h:h + 1, :]
        carry = cum[CUM_CHUNK - 1:CUM_CHUNK, :]


def _fox_prep(tail, bias_row, batch, seq):
    m = tail.shape[0]
    return pl.pallas_call(
        functools.partial(_fox_prep_kernel, seq=seq),
        grid=(batch,),
        in_specs=[
            pl.BlockSpec((seq, HEAD_DIM), lambda b: (b, 1)),
            pl.BlockSpec((1, HEAD_DIM), lambda b: (0, 0)),
        ],
        out_specs=[
            pl.BlockSpec((seq, HEAD_DIM), lambda b: (b, 0)),
            pl.BlockSpec((1, N_HEADS, 1, seq), lambda b: (b, 0, 0, 0)),
        ],
        out_shape=[
            jax.ShapeDtypeStruct((m, HEAD_DIM), F32),
            jax.ShapeDtypeStruct((batch, N_HEADS, 1, seq), F32),
        ],
        compiler_params=_params("parallel"),
        name="fox_prep",
    )(tail, bias_row)


def _flash_kernel(*refs, mla):
    if mla:
        q_ref, k_ref, kpe_ref, v_ref, o_ref, m_s, acc_s = refs
    else:
        q_ref, k_ref, v_ref, ccol_ref, crow_ref, o_ref, m_s, acc_s, ct_s = refs
    fox = not mla
    t = TK_ATT
    n_sub = TQ_ATT // t
    reps = t // HEAD_DIM
    head = pl.program_id(1)
    qi = pl.program_id(2)
    ones = jnp.ones((t, HEAD_DIM), BF16)
    m_s[...] = jnp.full(m_s.shape, NEG_INF, F32)
    acc_s[...] = jnp.zeros(acc_s.shape, F32)
    causal = (lax.broadcasted_iota(jnp.int32, (t, t), 1) <= lax.broadcasted_iota(jnp.int32, (t, t), 0))
    if fox:
        lane = lax.broadcasted_iota(jnp.int32, (TQ_ATT, HEAD_DIM), 1)
        c_t = jnp.sum(jnp.where(lane == head, ccol_ref[...], 0.0), axis=1, keepdims=True)
        ct_s[...] = jnp.broadcast_to(c_t * LOG2E, (TQ_ATT, HEAD_DIM))

    def chunk(start, subs):
        k = k_ref[pl.ds(start, t), :]
        if mla:
            k = jnp.concatenate([k, kpe_ref[pl.ds(start, t), :]], axis=1)
        v_aug = jnp.concatenate([v_ref[pl.ds(start, t), :], ones], axis=1)
        rows = [slice(sub * t, (sub + 1) * t) for sub, _ in subs]
        scores = [_dot_nt(q_ref[r, :], k) for r in rows]
        if fox:
            c_s = crow_ref[0, 0, :, pl.ds(start, t)] * LOG2E
        probs, alphas = [], []
        for r, s, (_, diagonal) in zip(rows, scores, subs):
            if fox:
                s = s - c_s
            if diagonal:
                s = jnp.where(causal, s, NEG_INF)
            m_prev = m_s[r, :]
            top = jnp.max(s, axis=-1, keepdims=True)
            if fox:
                top = top + ct_s[r, :]
            m_new = jnp.maximum(m_prev, top)
            m_s[r, :] = m_new
            alphas.append(jnp.exp2(m_prev - m_new))
            shift = m_new - ct_s[r, :] if fox else m_new
            probs.append(jnp.exp2(s - _lane_tile(shift, reps)).astype(BF16))
        for r, p, alpha in zip(rows, probs, alphas):
            acc_s[r, :] = _lane_tile(alpha, 2) * acc_s[r, :] + _dot(p, v_aug)

    def body(c, carry):
        chunk(pl.multiple_of(c * t, t), tuple((sub, False) for sub in range(n_sub)))
        return carry

    lax.fori_loop(0, qi * n_sub, body, 0)
    base = qi * TQ_ATT
    for j in range(n_sub):
        chunk(pl.multiple_of(base + j * t, t), tuple((sub, sub == j) for sub in range(j, n_sub)))
    acc = acc_s[...]
    o_ref[...] = (acc[:, :HEAD_DIM] / acc[:, HEAD_DIM:]).astype(o_ref.dtype)


def _flash_scratch(fox):
    stats = [pltpu.VMEM((TQ_ATT, HEAD_DIM), F32), pltpu.VMEM((TQ_ATT, 2 * HEAD_DIM), F32)]
    return stats + ([pltpu.VMEM((TQ_ATT, HEAD_DIM), F32)] if fox else [])


def _mla_attention(q_a, kv_a, kpe, batch, seq):
    m = q_a.shape[0]
    nq = seq // TQ_ATT
    return pl.pallas_call(
        functools.partial(_flash_kernel, mla=True),
        grid=(batch, N_HEADS, nq),
        in_specs=[
            pl.BlockSpec((TQ_ATT, MLA_HEAD_PAD), lambda b, h, i: (b * nq + i, h)),
            pl.BlockSpec((seq, HEAD_DIM), lambda b, h, i: (b, h)),
            pl.BlockSpec((seq, HEAD_DIM), lambda b, h, i: (b, 0)),
            pl.BlockSpec((seq, HEAD_DIM), lambda b, h, i: (b, N_HEADS + h)),
        ],
        out_specs=pl.BlockSpec((TQ_ATT, HEAD_DIM), lambda b, h, i: (b * nq + i, h)),
        out_shape=jax.ShapeDtypeStruct((m, GROUP_WIDTH), BF16),
        scratch_shapes=_flash_scratch(False),
        compiler_params=_params("parallel", "parallel", "arbitrary"),
        name="mla_attention",
    )(q_a, kv_a, kpe, kv_a)


def _fox_attention(proj, ccol, crow, batch, seq):
    m = proj.shape[0]
    nq = seq // TQ_ATT
    return pl.pallas_call(
        functools.partial(_flash_kernel, mla=False),
        grid=(batch, N_HEADS, nq),
        in_specs=[
            pl.BlockSpec((TQ_ATT, HEAD_DIM), lambda b, h, i: (b * nq + i, COL_QC + h)),
            pl.BlockSpec((seq, HEAD_DIM), lambda b, h, i: (b, COL_KC + h)),
            pl.BlockSpec((seq, HEAD_DIM), lambda b, h, i: (b, COL_VC + h)),
            pl.BlockSpec((TQ_ATT, HEAD_DIM), lambda b, h, i: (b * nq + i, 0)),
            pl.BlockSpec((1, 1, 1, seq), lambda b, h, i: (b, h, 0, 0)),
        ],
        out_specs=pl.BlockSpec((TQ_ATT, HEAD_DIM), lambda b, h, i: (b * nq + i, h)),
        out_shape=jax.ShapeDtypeStruct((m, GROUP_WIDTH), BF16),
        scratch_shapes=_flash_scratch(True),
        compiler_params=_params("parallel", "parallel", "arbitrary"),
        name="fox_attention",
    )(proj, proj, proj, ccol, crow)


def _sb_kernel(q_ref, k_ref, v_ref, o_ref, r_s, acc_s):
    t = T_SB
    n_sub = TQ_SB // t
    reps = t // HEAD_DIM
    first = pl.program_id(2) * n_sub
    row = lax.broadcasted_iota(jnp.int32, (t, t), 0)
    col = lax.broadcasted_iota(jnp.int32, (t, t), 1)
    later = jnp.where(row > col, 1.0, 0.0).astype(BF16)
    past = col < row
    r_s[...] = jnp.zeros(r_s.shape, F32)
    acc_s[...] = jnp.zeros(acc_s.shape, F32)

    def attend(dist, diagonal):
        rows = [slice(sub * t, (sub + 1) * t) for sub in range(n_sub)]
        chunk_ids = [first + (sub - dist) for sub in range(n_sub)]
        always = [isinstance(dist, int) and sub - dist >= 0 for sub in range(n_sub)]
        starts = [pl.multiple_of(jnp.maximum(c, 0) * t, t) for c in chunk_ids]
        logits = [_dot_nt(q_ref[r, :], k_ref[pl.ds(st, t), :]) for r, st in zip(rows, starts)]
        log_betas, log_keeps = [], []
        for z, c, sure in zip(logits, chunk_ids, always):
            neg_abs = pltpu.bitcast(pltpu.bitcast(z, jnp.uint32) | jnp.uint32(0x80000000), F32)
            soft = jnp.log2(1.0 + jnp.exp2(neg_abs))
            log_beta = jnp.minimum(z, 0.0) - soft
            log_keep = log_beta - z
            if diagonal:
                log_keep = jnp.where(past, log_keep, 0.0)
            if not sure:
                log_keep = jnp.where(c >= 0, log_keep, 0.0)
            log_betas.append(log_beta)
            log_keeps.append(log_keep)
        betweens = [_dot(lk.astype(BF16), later) for lk in log_keeps]
        weights = []
        for r, log_beta, between, c, sure in zip(rows, log_betas, betweens, chunk_ids, always):
            a = jnp.exp2(log_beta + (between + _lane_tile(r_s[r, :], reps)))
            if diagonal:
                a = jnp.where(past, a, 0.0)
            if not sure:
                a = jnp.where(c >= 0, a, 0.0)
            weights.append(a.astype(BF16))
        for r, a, log_keep, st in zip(rows, weights, log_keeps, starts):
            acc_s[r, :] += _dot(a, v_ref[pl.ds(st, t), :])
            r_s[r, :] += jnp.sum(log_keep, axis=-1, keepdims=True)

    def live_max(dist):
        tops = [jnp.where(first + sub - dist >= 0, jnp.max(r_s[sub * t:(sub + 1) * t, :]), SB_DEAD_LOG2)
                for sub in range(n_sub)]
        return functools.reduce(jnp.maximum, tops)

    attend(0, True)
    attend(1, False)

    def more(carry):
        return carry[1] > SB_DEAD_LOG2

    def body(carry):
        dist = carry[0]
        attend(dist, False)
        return dist + 1, live_max(dist + 1)

    lax.while_loop(more, body, (jnp.int32(2), live_max(2)))
    o_ref[...] = acc_s[...].astype(o_ref.dtype)


def _sb_attention(proj, batch, seq):
    m = proj.shape[0]
    nq = seq // TQ_SB
    return pl.pallas_call(
        _sb_kernel,
        grid=(batch, N_HEADS, nq),
        in_specs=[
            pl.BlockSpec((TQ_SB, HEAD_DIM), lambda b, h, i: (b * nq + i, COL_QD + h)),
            pl.BlockSpec((seq, HEAD_DIM), lambda b, h, i: (b, COL_KD + h)),
            pl.BlockSpec((seq, HEAD_DIM), lambda b, h, i: (b, COL_VD + h)),
        ],
        out_specs=pl.BlockSpec((TQ_SB, HEAD_DIM), lambda b, h, i: (b * nq + i, h)),
        out_shape=jax.ShapeDtypeStruct((m, GROUP_WIDTH), BF16),
        scratch_shapes=[pltpu.VMEM((TQ_SB, HEAD_DIM), F32), pltpu.VMEM((TQ_SB, HEAD_DIM), F32)],
        compiler_params=_params("parallel", "parallel", "arbitrary"),
        name="sb_attention",
    )(proj, proj, proj)


def _dilated_kernel(q_ref, k_ref, v_ref, cos_ref, sin_ref, o_ref, q_s, k_s, v_s, og_s, lg_s, *, seq):
    for c in range(seq // ROWS_DIL):
        rows = slice(c * ROWS_DIL, (c + 1) * ROWS_DIL)
        cos = cos_ref[rows, :]
        sin = sin_ref[rows, :]
        q_s[rows, :] = _rope128(q_ref[rows, :].astype(F32), cos, sin)
        k_s[rows, :] = _rope128(k_ref[rows, :].astype(F32), cos, sin)
        v_s[rows, :] = v_ref[rows, :].astype(F32)

    ones = jnp.ones((2 * BAND, HEAD_DIM), BF16)
    row1 = lax.broadcasted_iota(jnp.int32, (BAND, BAND), 0)
    col1 = lax.broadcasted_iota(jnp.int32, (BAND, BAND), 1)
    valid1 = col1 <= row1
    row2 = lax.broadcasted_iota(jnp.int32, (BAND, 2 * BAND), 0)
    col2 = lax.broadcasted_iota(jnp.int32, (BAND, 2 * BAND), 1)
    dist = BAND + row2 - col2
    valid2 = (dist >= 0) & (dist <= BAND)

    def attend(g, d, q_start, k_start, n_keys, valid):
        def strided(start, n):
            return pl.ds(start, n, stride=d) if d > 1 else pl.ds(start, n)
        qb = q_s[strided(q_start, BAND), :].astype(BF16)
        kb = k_s[strided(k_start, n_keys), :].astype(BF16)
        vb = v_s[strided(k_start, n_keys), :].astype(BF16)
        s = jnp.where(valid, _dot_nt(qb, kb), NEG_INF)
        m = jnp.max(s, axis=-1, keepdims=True)
        p = jnp.exp2(s - m)
        pv = _dot(p.astype(BF16), jnp.concatenate([vb, ones[:n_keys]], axis=1))
        den = pv[:, HEAD_DIM:]
        og_s[g, strided(q_start, BAND), :] = pv[:, :HEAD_DIM] / den
        lg_s[g, strided(q_start, BAND), :] = m + jnp.log2(den)

    for g, d in enumerate(DILATIONS):
        nb = seq // d // BAND

        def residue(r, carry, g=g, d=d, nb=nb):
            attend(g, d, r, r, BAND, valid1)

            def block(i, c2):
                attend(g, d, r + i * (BAND * d), r + (i - 1) * (BAND * d), 2 * BAND, valid2)
                return c2

            lax.fori_loop(1, nb, block, 0, unroll=4 if nb > 2 else 1)
            return carry

        lax.fori_loop(0, d, residue, 0, unroll=4 if (nb <= 2 and d > 1) else 1)

    for c in range(seq // ROWS_DIL):
        rows = slice(c * ROWS_DIL, (c + 1) * ROWS_DIL)
        l1, l2, l3 = lg_s[0, rows, :], lg_s[1, rows, :], lg_s[2, rows, :]
        top = jnp.maximum(jnp.maximum(l1, l2), l3)
        e1, e2, e3 = jnp.exp2(l1 - top), jnp.exp2(l2 - top), jnp.exp2(l3 - top)
        num = (e1 * og_s[0, rows, :] + e2 * og_s[1, rows, :]) + e3 * og_s[2, rows, :]
        o_ref[rows, :] = (num / ((e1 + e2) + e3)).astype(o_ref.dtype)


def _dilated_attention(proj, cos_b, sin_b, batch, seq):
    m = proj.shape[0]
    blk = (seq, HEAD_DIM)
    n_groups = len(DILATIONS)
    return pl.pallas_call(
        functools.partial(_dilated_kernel, seq=seq),
        grid=(batch, N_HEADS),
        in_specs=[
            pl.BlockSpec(blk, lambda b, h: (b, COL_QB + h)),
            pl.BlockSpec(blk, lambda b, h: (b, COL_KB + h)),
            pl.BlockSpec(blk, lambda b, h: (b, COL_VB + h)),
            pl.BlockSpec(blk, lambda b, h: (0, 0)),
            pl.BlockSpec(blk, lambda b, h: (0, 0)),
        ],
        out_specs=pl.BlockSpec(blk, lambda b, h: (b, h)),
        out_shape=jax.ShapeDtypeStruct((m, GROUP_WIDTH), BF16),
        scratch_shapes=[pltpu.VMEM(blk, F32), pltpu.VMEM(blk, F32), pltpu.VMEM(blk, F32),
                        pltpu.VMEM((n_groups,) + blk, F32), pltpu.VMEM((n_groups,) + blk, F32)],
        compiler_params=_params("parallel", "parallel"),
        name="dilated_attention",
    )(proj, proj, proj, cos_b, sin_b)


def _out_proj_kernel(oa_ref, ob_ref, oc_ref, od_ref, gn_ref, w_ref, x_ref, o_ref):
    for sub in range(TM_OUT // ROWS_OUT):
        rows = slice(sub * ROWS_OUT, (sub + 1) * ROWS_OUT)
        parts = []
        for group, ref in enumerate((oa_ref, ob_ref, oc_ref, od_ref)):
            cols = slice(group * GROUP_WIDTH, (group + 1) * GROUP_WIDTH)
            parts.append(_rms(ref[rows, :].astype(F32), gn_ref[:, cols]).astype(BF16))
        g = jnp.concatenate(parts, axis=1)
        o_ref[rows, :] = x_ref[rows, :] + _dot(g, w_ref[...])


def _out_proj(oa, ob, oc, od, gn, w_out, x):
    m = x.shape[0]
    grp = pl.BlockSpec((TM_OUT, GROUP_WIDTH), lambda i: (i, 0))
    return pl.pallas_call(
        _out_proj_kernel,
        grid=(m // TM_OUT,),
        in_specs=[
            grp, grp, grp, grp,
            pl.BlockSpec((1, D_MODEL), lambda i: (0, 0)),
            pl.BlockSpec((D_MODEL, D_MODEL), lambda i: (0, 0)),
            pl.BlockSpec((TM_OUT, D_MODEL), lambda i: (i, 0)),
        ],
        out_specs=pl.BlockSpec((TM_OUT, D_MODEL), lambda i: (i, 0)),
        out_shape=jax.ShapeDtypeStruct((m, D_MODEL), F32),
        compiler_params=_params("parallel"),
        name="out_proj",
    )(oa, ob, oc, od, gn, w_out, x)


def _ffn_kernel(x_ref, g_ref, wg_ref, wu_ref, wd_ref, fg_ref, o_ref, h_s, *, final):
    f = pl.program_id(1)

    @pl.when(f == 0)
    def _():
        x = x_ref[...]
        h_s[...] = _rms(x, g_ref[...]).astype(BF16)
        o_ref[...] = x

    h = h_s[...]
    gate = _dot(h, wg_ref[...])
    up = _dot(h, wu_ref[...])
    act = (gate / (1.0 + jnp.exp(-gate))) * up
    o_ref[...] += _dot(act.astype(BF16), wd_ref[...])

    if final:
        @pl.when(f == pl.num_programs(1) - 1)
        def _():
            o_ref[...] = _rms(o_ref[...], fg_ref[...])


def _ffn(x, gain, w_gate, w_up, w_down, final_gain, final):
    m = x.shape[0]
    return pl.pallas_call(
        functools.partial(_ffn_kernel, final=final),
        grid=(m // TM_FFN, FFN_HIDDEN // TF_FFN),
        in_specs=[
            pl.BlockSpec((TM_FFN, D_MODEL), lambda i, f: (i, 0)),
            pl.BlockSpec((1, D_MODEL), lambda i, f: (0, 0)),
            pl.BlockSpec((D_MODEL, TF_FFN), lambda i, f: (0, f)),
            pl.BlockSpec((D_MODEL, TF_FFN), lambda i, f: (0, f)),
            pl.BlockSpec((TF_FFN, D_MODEL), lambda i, f: (f, 0)),
            pl.BlockSpec((1, D_MODEL), lambda i, f: (0, 0)),
        ],
        out_specs=pl.BlockSpec((TM_FFN, D_MODEL), lambda i, f: (i, 0)),
        out_shape=jax.ShapeDtypeStruct((m, D_MODEL), F32),
        scratch_shapes=[pltpu.VMEM((TM_FFN, D_MODEL), BF16)],
        compiler_params=_params("parallel", "arbitrary"),
        name="ffn_final" if final else "ffn",
    )(x, gain, w_gate, w_up, w_down, final_gain)


def _rot_cols(w):
    half = w.shape[-1] // 2
    return jnp.concatenate([-w[..., half:], w[..., :half]], axis=-1)


def _prep_weights(w_in, w_uq, w_ukv, fox_forget_bias):
    depth = w_in.shape[0]
    col_scale = np.ones((IN_OFFSETS[-1],), np.float32)
    for q_split in (3, 6, 10):
        col_scale[IN_OFFSETS[q_split]:IN_OFFSETS[q_split + 1]] = HEAD_DIM ** -0.5 * LOG2E
    w = (w_in * col_scale).astype(BF16)
    k_rope = w[..., IN_OFFSETS[2]:IN_OFFSETS[3]]
    forget = w[..., IN_OFFSETS[9]:IN_OFFSETS[10]]
    w_main = jnp.concatenate([w[..., :IN_OFFSETS[2]], w[..., IN_OFFSETS[3]:IN_OFFSETS[9]],
                              w[..., IN_OFFSETS[10]:]], axis=-1)
    w_tail = jnp.concatenate([k_rope, _rot_cols(k_rope), forget,
                              jnp.zeros((depth, D_MODEL, HEAD_DIM - N_HEADS), BF16)], axis=-1)
    uq = (w_uq * (MLA_QK_DIM ** -0.5 * LOG2E)).reshape(depth, Q_LORA, N_HEADS, MLA_QK_DIM)
    pe = uq[..., QK_NOPE:]
    uq = jnp.concatenate([uq[..., :QK_NOPE], pe, _rot_cols(pe)], axis=-1)
    uq = uq.reshape(depth, Q_LORA, N_HEADS * MLA_HEAD_PAD)
    ukv = w_ukv.reshape(depth, KV_LORA, N_HEADS, 2, HEAD_DIM).transpose(0, 1, 3, 2, 4)
    ukv = ukv.reshape(depth, KV_LORA, 2 * GROUP_WIDTH)
    bias_rows = jnp.pad(fox_forget_bias, ((0, 0), (0, HEAD_DIM - N_HEADS)))[:, None, :]
    return w_main, w_tail, uq.astype(BF16), ukv.astype(BF16), bias_rows


def _rope_tables(seq):
    pos = jnp.arange(seq, dtype=F32)[:, None]

    def angles(dim):
        inv_freq = ROPE_THETA ** (-jnp.arange(0, dim, 2, dtype=F32) / dim)
        return pos * inv_freq[None, :]

    ang_b = angles(HEAD_DIM)
    cos_b = jnp.concatenate([jnp.cos(ang_b)] * 2, axis=1)
    sin_b = jnp.concatenate([-jnp.sin(ang_b), jnp.sin(ang_b)], axis=1)
    ang_a = angles(QK_ROPE)
    zeros = jnp.zeros((seq, HEAD_DIM - QK_ROPE), F32)
    cos_a = jnp.concatenate([jnp.cos(ang_a)] * 2 + [zeros], axis=1)
    sin_a = jnp.concatenate([jnp.sin(ang_a)] * 2 + [zeros], axis=1)
    return cos_a, sin_a, cos_b, sin_b


def kernel(x, attn_norm, w_in, mla_q_norm, w_uq, mla_kv_norm, w_ukv, fox_forget_bias, group_norm,
           w_out, ffn_norm, w_gate, w_up, w_down, final_norm):
    batch, seq, _ = x.shape
    m = batch * seq
    assert seq % TQ_ATT == 0 and seq % (max(DILATIONS) * BAND) == 0 and m % TM_IN == 0
    w_main, w_tail, uq, ukv, bias_rows = _prep_weights(w_in, w_uq, w_ukv, fox_forget_bias)
    w_out_b, w_gate_b, w_up_b, w_down_b = (w.astype(BF16) for w in (w_out, w_gate, w_up, w_down))
    cos_a, sin_a, cos_b, sin_b = _rope_tables(seq)
    row = lambda v: v.reshape(1, -1)

    xf = x.reshape(m, D_MODEL)
    for l in range(DEPTH):
        proj, tail = _in_proj(xf, row(attn_norm[l]), w_main[l], w_tail[l])
        q_a, kv_a, kpe = _mla_prep(proj, tail, row(mla_q_norm[l]), row(mla_kv_norm[l]), uq[l], ukv[l],
                                   cos_a, sin_a, seq)
        ccol, crow = _fox_prep(tail, bias_rows[l], batch, seq)
        out_a = _mla_attention(q_a, kv_a, kpe, batch, seq)
        out_b = _dilated_attention(proj, cos_b, sin_b, batch, seq)
        out_c = _fox_attention(proj, ccol, crow, batch, seq)
        out_d = _sb_attention(proj, batch, seq)
        xf = _out_proj(out_a, out_b, out_c, out_d, row(group_norm[l]), w_out_b[l], xf)
        xf = _ffn(xf, row(ffn_norm[l]), w_gate_b[l], w_up_b[l], w_down_b[l], row(final_norm),
                  final=(l == DEPTH - 1))
    return xf.reshape(batch, seq, D_MODEL)
```

```python
import functools
import math

import jax
import jax.numpy as jnp
import numpy as np
from jax import lax
from jax.experimental import pallas as pl
from jax.experimental.pallas import tpu as pltpu

F32 = jnp.float32
BF16 = jnp.bfloat16

D_MODEL = 2048
DEPTH = 4
N_HEADS = 4
HEAD_DIM = 128
GROUP_WIDTH = N_HEADS * HEAD_DIM
Q_LORA = 512
KV_LORA = 512
QK_NOPE = 128
QK_ROPE = 64
MLA_QK_DIM = QK_NOPE + QK_ROPE
MLA_HEAD_PAD = 2 * HEAD_DIM
DILATED_PAIRS = ((128, 1), (512, 4), (2048, 16))
BAND = 128
assert all(window // dilation == BAND for window, dilation in DILATED_PAIRS)
DILATIONS = tuple(dilation for _, dilation in DILATED_PAIRS)
ROPE_THETA = 10000.0
FFN_HIDDEN = 5632
EPS = 1e-6
NEG_INF = -1e30
LOG2E = math.log2(math.e)

IN_SPLITS = ((Q_LORA, KV_LORA, QK_ROPE) + (GROUP_WIDTH,) * 3 + (GROUP_WIDTH,) * 3 + (N_HEADS,)
             + (GROUP_WIDTH,) * 3)
IN_OFFSETS = tuple(int(v) for v in np.cumsum((0,) + IN_SPLITS))

MAIN_WIDTH = Q_LORA + KV_LORA + 9 * GROUP_WIDTH
TAIL_WIDTH = 2 * HEAD_DIM
COL_QB, COL_KB, COL_VB = 8, 12, 16
COL_QC, COL_KC, COL_VC = 20, 24, 28
COL_QD, COL_KD, COL_VD = 32, 36, 40

VMEM_LIMIT_BYTES = 56 * 1024 * 1024
TM_IN, TN_IN = 512, 512
TM_MLA = 1024
TM_OUT = 512
ROWS_OUT = 128
TM_FFN, TF_FFN = 1024, 512
TQ_ATT, TK_ATT = 4096, 512
TQ_SB, T_SB = 1024, 256
SB_DEAD_LOG2 = -150.0
ROWS_DIL = 512
CUM_CHUNK = 256


def _params(*sem):
    return pltpu.CompilerParams(dimension_semantics=sem, vmem_limit_bytes=VMEM_LIMIT_BYTES)


def _rms(x, gain):
    return (x * lax.rsqrt(jnp.mean(x * x, axis=-1, keepdims=True) + EPS)) * gain


def _dot(a, b):
    return jnp.dot(a, b, preferred_element_type=F32)


def _dot_nt(a, b):
    return lax.dot_general(a, b, (((1,), (1,)), ((), ())), preferred_element_type=F32)


def _lane_tile(x, n):
    return jnp.concatenate([x] * n, axis=1)


def _rope128(y, cos, sin):
    return y * cos + pltpu.roll(y, 64, 1) * sin


def _in_proj_kernel(x_ref, g_ref, w_ref, wt_ref, o_ref, ot_ref):
    hb = _rms(x_ref[...], g_ref[...]).astype(BF16)
    ot_ref[...] = _dot(hb, wt_ref[...])
    for j in range(MAIN_WIDTH // TN_IN):
        cols = slice(j * TN_IN, (j + 1) * TN_IN)
        o_ref[:, cols] = _dot(hb, w_ref[:, cols]).astype(o_ref.dtype)


def _in_proj(x, gain, w_main, w_tail, layer):
    m = x.shape[0]
    resident = dict(pipeline_mode=pl.Buffered(1))
    return pl.pallas_call(
        _in_proj_kernel,
        grid=(m // TM_IN,),
        in_specs=[
            pl.BlockSpec((TM_IN, D_MODEL), lambda i: (i, 0)),
            pl.BlockSpec((1, D_MODEL), lambda i: (0, 0)),
            pl.BlockSpec((None, D_MODEL, MAIN_WIDTH), lambda i: (layer, 0, 0), **resident),
            pl.BlockSpec((None, D_MODEL, TAIL_WIDTH), lambda i: (layer, 0, 0), **resident),
        ],
        out_specs=[
            pl.BlockSpec((TM_IN, MAIN_WIDTH), lambda i: (i, 0)),
            pl.BlockSpec((TM_IN, TAIL_WIDTH), lambda i: (i, 0)),
        ],
        out_shape=[
            jax.ShapeDtypeStruct((m, MAIN_WIDTH), BF16),
            jax.ShapeDtypeStruct((m, TAIL_WIDTH), F32),
        ],
        compiler_params=_params("parallel"),
        name="in_proj",
    )(x, gain, w_main, w_tail)


def _mla_prep_kernel(ql_ref, kvl_ref, t_ref, gq_ref, gkv_ref, wq_ref, wkv_ref, cos_ref, sin_ref,
                     q_ref, kv_ref, kpe_ref):
    cos = cos_ref[...]
    sin = sin_ref[...]
    hq = _rms(ql_ref[...].astype(F32), gq_ref[...]).astype(BF16)
    q = _dot(hq, wq_ref[...])
    for h in range(N_HEADS):
        lo = h * MLA_HEAD_PAD
        q_ref[:, lo:lo + HEAD_DIM] = q[:, lo:lo + HEAD_DIM].astype(BF16)
        y = q[:, lo + HEAD_DIM:lo + MLA_HEAD_PAD]
        q_ref[:, lo + HEAD_DIM:lo + MLA_HEAD_PAD] = _rope128(y, cos, sin).astype(BF16)
    hkv = _rms(kvl_ref[...].astype(F32), gkv_ref[...]).astype(BF16)
    kv_ref[...] = _dot(hkv, wkv_ref[...]).astype(BF16)
    kpe_ref[...] = _rope128(t_ref[...], cos, sin).astype(BF16)


def _mla_prep(proj, tail, gq, gkv, wq, wkv, layer, cos_t, sin_t, seq):
    m = proj.shape[0]
    nseq = seq // TM_MLA
    return pl.pallas_call(
        _mla_prep_kernel,
        grid=(m // TM_MLA,),
        in_specs=[
            pl.BlockSpec((TM_MLA, Q_LORA), lambda i: (i, 0)),
            pl.BlockSpec((TM_MLA, KV_LORA), lambda i: (i, 1)),
            pl.BlockSpec((TM_MLA, HEAD_DIM), lambda i: (i, 0)),
            pl.BlockSpec((1, Q_LORA), lambda i: (0, 0)),
            pl.BlockSpec((1, KV_LORA), lambda i: (0, 0)),
            pl.BlockSpec((None, Q_LORA, N_HEADS * MLA_HEAD_PAD), lambda i: (layer, 0, 0)),
            pl.BlockSpec((None, KV_LORA, 2 * GROUP_WIDTH), lambda i: (layer, 0, 0)),
            pl.BlockSpec((TM_MLA, HEAD_DIM), lambda i: (i % nseq, 0)),
            pl.BlockSpec((TM_MLA, HEAD_DIM), lambda i: (i % nseq, 0)),
        ],
        out_specs=[
            pl.BlockSpec((TM_MLA, N_HEADS * MLA_HEAD_PAD), lambda i: (i, 0)),
            pl.BlockSpec((TM_MLA, 2 * GROUP_WIDTH), lambda i: (i, 0)),
            pl.BlockSpec((TM_MLA, HEAD_DIM), lambda i: (i, 0)),
        ],
        out_shape=[
            jax.ShapeDtypeStruct((m, N_HEADS * MLA_HEAD_PAD), BF16),
            jax.ShapeDtypeStruct((m, 2 * GROUP_WIDTH), BF16),
            jax.ShapeDtypeStruct((m, HEAD_DIM), BF16),
        ],
        compiler_params=_params("parallel"),
        name="mla_prep",
    )(proj, proj, tail, gq, gkv, wq, wkv, cos_t, sin_t)


def _split3(x):
    hi = x.astype(BF16)
    r1 = x - hi.astype(F32)
    mid = r1.astype(BF16)
    lo = (r1 - mid.astype(F32)).astype(BF16)
    return hi, mid, lo


def _fox_prep_kernel(t_ref, b_ref, col_ref, row_ref, *, seq):
    r = lax.broadcasted_iota(jnp.int32, (CUM_CHUNK, CUM_CHUNK), 0)
    c = lax.broadcasted_iota(jnp.int32, (CUM_CHUNK, CUM_CHUNK), 1)
    lower = jnp.where(c <= r, 1.0, 0.0).astype(BF16)
    carry = jnp.zeros((1, HEAD_DIM), F32)
    for ci in range(seq // CUM_CHUNK):
        rows = slice(ci * CUM_CHUNK, (ci + 1) * CUM_CHUNK)
        x = t_ref[rows, :] + b_ref[...]
        log_f = jnp.minimum(x, 0.0) - jnp.log(1.0 + jnp.exp(-jnp.abs(x)))
        hi, mid, lo = _split3(log_f)
        cum = (_dot(lower, hi) + _dot(lower, mid)) + _dot(lower, lo) + carry
        col_ref[rows, :] = cum
        cum_t = cum.T
        for h in range(N_HEADS):
            row_ref[0, h, :, rows] = cum_t[h:h + 1, :]
        carry = cum[CUM_CHUNK - 1:CUM_CHUNK, :]


def _fox_prep(tail, bias_row, batch, seq):
    m = tail.shape[0]
    return pl.pallas_call(
        functools.partial(_fox_prep_kernel, seq=seq),
        grid=(batch,),
        in_specs=[
            pl.BlockSpec((seq, HEAD_DIM), lambda b: (b, 1)),
            pl.BlockSpec((1, HEAD_DIM), lambda b: (0, 0)),
        ],
        out_specs=[
            pl.BlockSpec((seq, HEAD_DIM), lambda b: (b, 0)),
            pl.BlockSpec((1, N_HEADS, 1, seq), lambda b: (b, 0, 0, 0)),
        ],
        out_shape=[
            jax.ShapeDtypeStruct((m, HEAD_DIM), F32),
            jax.ShapeDtypeStruct((batch, N_HEADS, 1, seq), F32),
        ],
        compiler_params=_params("parallel"),
        name="fox_prep",
    )(tail, bias_row)


def _flash_kernel(*refs, mla):
    if mla:
        q_ref, k_ref, kpe_ref, v_ref, o_ref, m_s, acc_s = refs
    else:
        q_ref, k_ref, v_ref, ccol_ref, crow_ref, o_ref, m_s, acc_s, ct_s = refs
    fox = not mla
    t = TK_ATT
    n_sub = TQ_ATT // t
    reps = t // HEAD_DIM
    head = pl.program_id(1)
    qi = pl.program_id(2)
    ones = jnp.ones((t, HEAD_DIM), BF16)
    m_s[...] = jnp.full(m_s.shape, NEG_INF, F32)
    acc_s[...] = jnp.zeros(acc_s.shape, F32)
    causal = (lax.broadcasted_iota(jnp.int32, (t, t), 1) <= lax.broadcasted_iota(jnp.int32, (t, t), 0))
    if fox:
        lane = lax.broadcasted_iota(jnp.int32, (TQ_ATT, HEAD_DIM), 1)
        c_t = jnp.sum(jnp.where(lane == head, ccol_ref[...], 0.0), axis=1, keepdims=True)
        ct_s[...] = jnp.broadcast_to(c_t * LOG2E, (TQ_ATT, HEAD_DIM))

    def chunk(start, subs):
        k = k_ref[pl.ds(start, t), :]
        if mla:
            k = jnp.concatenate([k, kpe_ref[pl.ds(start, t), :]], axis=1)
        v_aug = jnp.concatenate([v_ref[pl.ds(start, t), :], ones], axis=1)
        rows = [slice(sub * t, (sub + 1) * t) for sub, _ in subs]
        scores = [_dot_nt(q_ref[r, :], k) for r in rows]
        if fox:
            c_s = crow_ref[0, 0, :, pl.ds(start, t)] * LOG2E
        probs, alphas = [], []
        for r, s, (_, diagonal) in zip(rows, scores, subs):
            if fox:
                s = s - c_s
            if diagonal:
                s = jnp.where(causal, s, NEG_INF)
            m_prev = m_s[r, :]
            top = jnp.max(s, axis=-1, keepdims=True)
            if fox:
                top = top + ct_s[r, :]
            m_new = jnp.maximum(m_prev, top)
            m_s[r, :] = m_new
            alphas.append(jnp.exp2(m_prev - m_new))
            shift = m_new - ct_s[r, :] if fox else m_new
            probs.append(jnp.exp2(s - _lane_tile(shift, reps)).astype(BF16))
        for r, p, alpha in zip(rows, probs, alphas):
            acc_s[r, :] = _lane_tile(alpha, 2) * acc_s[r, :] + _dot(p, v_aug)

    def body(c, carry):
        chunk(pl.multiple_of(c * t, t), tuple((sub, False) for sub in range(n_sub)))
        return carry

    lax.fori_loop(0, qi * n_sub, body, 0)
    base = qi * TQ_ATT
    for j in range(n_sub):
        chunk(pl.multiple_of(base + j * t, t), tuple((sub, sub == j) for sub in range(j, n_sub)))
    acc = acc_s[...]
    o_ref[...] = (acc[:, :HEAD_DIM] / acc[:, HEAD_DIM:]).astype(o_ref.dtype)


def _flash_scratch(fox):
    stats = [pltpu.VMEM((TQ_ATT, HEAD_DIM), F32), pltpu.VMEM((TQ_ATT, 2 * HEAD_DIM), F32)]
    return stats + ([pltpu.VMEM((TQ_ATT, HEAD_DIM), F32)] if fox else [])


def _mla_attention(q_a, kv_a, kpe, batch, seq):
    m = q_a.shape[0]
    nq = seq // TQ_ATT
    return pl.pallas_call(
        functools.partial(_flash_kernel, mla=True),
        grid=(batch, N_HEADS, nq),
        in_specs=[
            pl.BlockSpec((TQ_ATT, MLA_HEAD_PAD), lambda b, h, i: (b * nq + i, h)),
            pl.BlockSpec((seq, HEAD_DIM), lambda b, h, i: (b, h)),
            pl.BlockSpec((seq, HEAD_DIM), lambda b, h, i: (b, 0)),
            pl.BlockSpec((seq, HEAD_DIM), lambda b, h, i: (b, N_HEADS + h)),
        ],
        out_specs=pl.BlockSpec((TQ_ATT, HEAD_DIM), lambda b, h, i: (b * nq + i, h)),
        out_shape=jax.ShapeDtypeStruct((m, GROUP_WIDTH), BF16),
        scratch_shapes=_flash_scratch(False),
        compiler_params=_params("parallel", "parallel", "arbitrary"),
        name="mla_attention",
    )(q_a, kv_a, kpe, kv_a)


def _fox_attention(proj, ccol, crow, batch, seq):
    m = proj.shape[0]
    nq = seq // TQ_ATT
    return pl.pallas_call(
        functools.partial(_flash_kernel, mla=False),
        grid=(batch, N_HEADS, nq),
        in_specs=[
            pl.BlockSpec((TQ_ATT, HEAD_DIM), lambda b, h, i: (b * nq + i, COL_QC + h)),
            pl.BlockSpec((seq, HEAD_DIM), lambda b, h, i: (b, COL_KC + h)),
            pl.BlockSpec((seq, HEAD_DIM), lambda b, h, i: (b, COL_VC + h)),
            pl.BlockSpec((TQ_ATT, HEAD_DIM), lambda b, h, i: (b * nq + i, 0)),
            pl.BlockSpec((1, 1, 1, seq), lambda b, h, i: (b, h, 0, 0)),
        ],
        out_specs=pl.BlockSpec((TQ_ATT, HEAD_DIM), lambda b, h, i: (b * nq + i, h)),
        out_shape=jax.ShapeDtypeStruct((m, GROUP_WIDTH), BF16),
        scratch_shapes=_flash_scratch(True),
        compiler_params=_params("parallel", "parallel", "arbitrary"),
        name="fox_attention",
    )(proj, proj, proj, ccol, crow)


def _sb_kernel(q_ref, k_ref, v_ref, o_ref, r_s, acc_s):
    t = T_SB
    n_sub = TQ_SB // t
    reps = t // HEAD_DIM
    first = pl.program_id(2) * n_sub
    row = lax.broadcasted_iota(jnp.int32, (t, t), 0)
    col = lax.broadcasted_iota(jnp.int32, (t, t), 1)
    later = jnp.where(row > col, 1.0, 0.0).astype(BF16)
    past = col < row
    r_s[...] = jnp.zeros(r_s.shape, F32)
    acc_s[...] = jnp.zeros(acc_s.shape, F32)

    def attend(dist, diagonal):
        rows = [slice(sub * t, (sub + 1) * t) for sub in range(n_sub)]
        chunk_ids = [first + (sub - dist) for sub in range(n_sub)]
        always = [isinstance(dist, int) and sub - dist >= 0 for sub in range(n_sub)]
        starts = [pl.multiple_of(jnp.maximum(c, 0) * t, t) for c in chunk_ids]
        logits = [_dot_nt(q_ref[r, :], k_ref[pl.ds(st, t), :]) for r, st in zip(rows, starts)]
        log_betas, log_keeps = [], []
        for z, c, sure in zip(logits, chunk_ids, always):
            neg_abs = pltpu.bitcast(pltpu.bitcast(z, jnp.uint32) | jnp.uint32(0x80000000), F32)
            soft = jnp.log2(1.0 + jnp.exp2(neg_abs))
            log_beta = jnp.minimum(z, 0.0) - soft
            log_keep = log_beta - z
            if diagonal:
                log_keep = jnp.where(past, log_keep, 0.0)
            if not sure:
                log_keep = jnp.where(c >= 0, log_keep, 0.0)
            log_betas.append(log_beta)
            log_keeps.append(log_keep)
        betweens = [_dot(lk.astype(BF16), later) for lk in log_keeps]
        weights = []
        for r, log_beta, between, c, sure in zip(rows, log_betas, betweens, chunk_ids, always):
            a = jnp.exp2(log_beta + (between + _lane_tile(r_s[r, :], reps)))
            if diagonal:
                a = jnp.where(past, a, 0.0)
            if not sure:
                a = jnp.where(c >= 0, a, 0.0)
            weights.append(a.astype(BF16))
        for r, a, log_keep, st in zip(rows, weights, log_keeps, starts):
            acc_s[r, :] += _dot(a, v_ref[pl.ds(st, t), :])
            r_s[r, :] += jnp.sum(log_keep, axis=-1, keepdims=True)

    def live_max(dist):
        tops = [jnp.where(first + sub - dist >= 0, jnp.max(r_s[sub * t:(sub + 1) * t, :]), SB_DEAD_LOG2)
                for sub in range(n_sub)]
        return functools.reduce(jnp.maximum, tops)

    attend(0, True)
    attend(1, False)

    def more(carry):
        return carry[1] > SB_DEAD_LOG2

    def body(carry):
        dist = carry[0]
        attend(dist, False)
        return dist + 1, live_max(dist + 1)

    lax.while_loop(more, body, (jnp.int32(2), live_max(2)))
    o_ref[...] = acc_s[...].astype(o_ref.dtype)


def _sb_attention(proj, batch, seq):
    m = proj.shape[0]
    nq = seq // TQ_SB
    return pl.pallas_call(
        _sb_kernel,
        grid=(batch, N_HEADS, nq),
        in_specs=[
            pl.BlockSpec((TQ_SB, HEAD_DIM), lambda b, h, i: (b * nq + i, COL_QD + h)),
            pl.BlockSpec((seq, HEAD_DIM), lambda b, h, i: (b, COL_KD + h)),
            pl.BlockSpec((seq, HEAD_DIM), lambda b, h, i: (b, COL_VD + h)),
        ],
        out_specs=pl.BlockSpec((TQ_SB, HEAD_DIM), lambda b, h, i: (b * nq + i, h)),
        out_shape=jax.ShapeDtypeStruct((m, GROUP_WIDTH), BF16),
        scratch_shapes=[pltpu.VMEM((TQ_SB, HEAD_DIM), F32), pltpu.VMEM((TQ_SB, HEAD_DIM), F32)],
        compiler_params=_params("parallel", "parallel", "arbitrary"),
        name="sb_attention",
    )(proj, proj, proj)


def _dilated_kernel(q_ref, k_ref, v_ref, cos_ref, sin_ref, o_ref, q_s, k_s, v_s, og_s, lg_s, *, seq):
    for c in range(seq // ROWS_DIL):
        rows = slice(c * ROWS_DIL, (c + 1) * ROWS_DIL)
        cos = cos_ref[rows, :]
        sin = sin_ref[rows, :]
        q_s[rows, :] = _rope128(q_ref[rows, :].astype(F32), cos, sin)
        k_s[rows, :] = _rope128(k_ref[rows, :].astype(F32), cos, sin)
        v_s[rows, :] = v_ref[rows, :].astype(F32)

    ones = jnp.ones((2 * BAND, HEAD_DIM), BF16)
    row1 = lax.broadcasted_iota(jnp.int32, (BAND, BAND), 0)
    col1 = lax.broadcasted_iota(jnp.int32, (BAND, BAND), 1)
    valid1 = col1 <= row1
    row2 = lax.broadcasted_iota(jnp.int32, (BAND, 2 * BAND), 0)
    col2 = lax.broadcasted_iota(jnp.int32, (BAND, 2 * BAND), 1)
    dist = BAND + row2 - col2
    valid2 = (dist >= 0) & (dist <= BAND)

    def attend(g, d, q_start, k_start, n_keys, valid):
        def strided(start, n):
            return pl.ds(start, n, stride=d) if d > 1 else pl.ds(start, n)
        qb = q_s[strided(q_start, BAND), :].astype(BF16)
        kb = k_s[strided(k_start, n_keys), :].astype(BF16)
        vb = v_s[strided(k_start, n_keys), :].astype(BF16)
        s = jnp.where(valid, _dot_nt(qb, kb), NEG_INF)
        m = jnp.max(s, axis=-1, keepdims=True)
        p = jnp.exp2(s - m)
        pv = _dot(p.astype(BF16), jnp.concatenate([vb, ones[:n_keys]], axis=1))
        den = pv[:, HEAD_DIM:]
        og_s[g, strided(q_start, BAND), :] = pv[:, :HEAD_DIM] / den
        lg_s[g, strided(q_start, BAND), :] = m + jnp.log2(den)

    for g, d in enumerate(DILATIONS):
        nb = seq // d // BAND

        def residue(r, carry, g=g, d=d, nb=nb):
            attend(g, d, r, r, BAND, valid1)

            def block(i, c2):
                attend(g, d, r + i * (BAND * d), r + (i - 1) * (BAND * d), 2 * BAND, valid2)
                return c2

            lax.fori_loop(1, nb, block, 0, unroll=4 if nb > 2 else 1)
            return carry

        lax.fori_loop(0, d, residue, 0, unroll=4 if (nb <= 2 and d > 1) else 1)

    for c in range(seq // ROWS_DIL):
        rows = slice(c * ROWS_DIL, (c + 1) * ROWS_DIL)
        l1, l2, l3 = lg_s[0, rows, :], lg_s[1, rows, :], lg_s[2, rows, :]
        top = jnp.maximum(jnp.maximum(l1, l2), l3)
        e1, e2, e3 = jnp.exp2(l1 - top), jnp.exp2(l2 - top), jnp.exp2(l3 - top)
        num = (e1 * og_s[0, rows, :] + e2 * og_s[1, rows, :]) + e3 * og_s[2, rows, :]
        o_ref[rows, :] = (num / ((e1 + e2) + e3)).astype(o_ref.dtype)


def _dilated_attention(proj, cos_b, sin_b, batch, seq):
    m = proj.shape[0]
    blk = (seq, HEAD_DIM)
    n_groups = len(DILATIONS)
    return pl.pallas_call(
        functools.partial(_dilated_kernel, seq=seq),
        grid=(batch, N_HEADS),
        in_specs=[
            pl.BlockSpec(blk, lambda b, h: (b, COL_QB + h)),
            pl.BlockSpec(blk, lambda b, h: (b, COL_KB + h)),
            pl.BlockSpec(blk, lambda b, h: (b, COL_VB + h)),
            pl.BlockSpec(blk, lambda b, h: (0, 0)),
            pl.BlockSpec(blk, lambda b, h: (0, 0)),
        ],
        out_specs=pl.BlockSpec(blk, lambda b, h: (b, h)),
        out_shape=jax.ShapeDtypeStruct((m, GROUP_WIDTH), BF16),
        scratch_shapes=[pltpu.VMEM(blk, F32), pltpu.VMEM(blk, F32), pltpu.VMEM(blk, F32),
                        pltpu.VMEM((n_groups,) + blk, F32), pltpu.VMEM((n_groups,) + blk, F32)],
        compiler_params=_params("parallel", "parallel"),
        name="dilated_attention",
    )(proj, proj, proj, cos_b, sin_b)


def _out_proj_kernel(oa_ref, ob_ref, oc_ref, od_ref, gn_ref, w_ref, x_ref, o_ref):
    for sub in range(TM_OUT // ROWS_OUT):
        rows = slice(sub * ROWS_OUT, (sub + 1) * ROWS_OUT)
        parts = []
        for group, ref in enumerate((oa_ref, ob_ref, oc_ref, od_ref)):
            cols = slice(group * GROUP_WIDTH, (group + 1) * GROUP_WIDTH)
            parts.append(_rms(ref[rows, :].astype(F32), gn_ref[:, cols]).astype(BF16))
        g = jnp.concatenate(parts, axis=1)
        o_ref[rows, :] = x_ref[rows, :] + _dot(g, w_ref[...])


def _out_proj(oa, ob, oc, od, gn, w_out, layer, x):
    m = x.shape[0]
    grp = pl.BlockSpec((TM_OUT, GROUP_WIDTH), lambda i: (i, 0))
    return pl.pallas_call(
        _out_proj_kernel,
        grid=(m // TM_OUT,),
        in_specs=[
            grp, grp, grp, grp,
            pl.BlockSpec((1, D_MODEL), lambda i: (0, 0)),
            pl.BlockSpec((None, D_MODEL, D_MODEL), lambda i: (layer, 0, 0)),
            pl.BlockSpec((TM_OUT, D_MODEL), lambda i: (i, 0)),
        ],
        out_specs=pl.BlockSpec((TM_OUT, D_MODEL), lambda i: (i, 0)),
        out_shape=jax.ShapeDtypeStruct((m, D_MODEL), F32),
        compiler_params=_params("parallel"),
        name="out_proj",
    )(oa, ob, oc, od, gn, w_out, x)


def _ffn_kernel(x_ref, g_ref, wg_ref, wu_ref, wd_ref, fg_ref, o_ref, h_s, *, final):
    f = pl.program_id(1)

    @pl.when(f == 0)
    def _():
        x = x_ref[...]
        h_s[...] = _rms(x, g_ref[...]).astype(BF16)
        o_ref[...] = x

    h = h_s[...]
    gate = _dot(h, wg_ref[...])
    up = _dot(h, wu_ref[...])
    act = (gate / (1.0 + jnp.exp(-gate))) * up
    o_ref[...] += _dot(act.astype(BF16), wd_ref[...])

    if final:
        @pl.when(f == pl.num_programs(1) - 1)
        def _():
            o_ref[...] = _rms(o_ref[...], fg_ref[...])


def _ffn(x, gain, w_gate, w_up, w_down, layer, final_gain, final):
    m = x.shape[0]
    return pl.pallas_call(
        functools.partial(_ffn_kernel, final=final),
        grid=(m // TM_FFN, FFN_HIDDEN // TF_FFN),
        in_specs=[
            pl.BlockSpec((TM_FFN, D_MODEL), lambda i, f: (i, 0)),
            pl.BlockSpec((1, D_MODEL), lambda i, f: (0, 0)),
            pl.BlockSpec((None, D_MODEL, TF_FFN), lambda i, f: (layer, 0, f)),
            pl.BlockSpec((None, D_MODEL, TF_FFN), lambda i, f: (layer, 0, f)),
            pl.BlockSpec((None, TF_FFN, D_MODEL), lambda i, f: (layer, f, 0)),
            pl.BlockSpec((1, D_MODEL), lambda i, f: (0, 0)),
        ],
        out_specs=pl.BlockSpec((TM_FFN, D_MODEL), lambda i, f: (i, 0)),
        out_shape=jax.ShapeDtypeStruct((m, D_MODEL), F32),
        scratch_shapes=[pltpu.VMEM((TM_FFN, D_MODEL), BF16)],
        compiler_params=_params("parallel", "arbitrary"),
        name="ffn_final" if final else "ffn",
    )(x, gain, w_gate, w_up, w_down, final_gain)


def _rot_cols(w):
    half = w.shape[-1] // 2
    return jnp.concatenate([-w[..., half:], w[..., :half]], axis=-1)


def _prep_weights(w_in, w_uq, w_ukv, fox_forget_bias):
    depth = w_in.shape[0]
    col_scale = np.ones((IN_OFFSETS[-1],), np.float32)
    for q_split in (3, 6, 10):
        col_scale[IN_OFFSETS[q_split]:IN_OFFSETS[q_split + 1]] = HEAD_DIM ** -0.5 * LOG2E
    w = (w_in * col_scale).astype(BF16)
    k_rope = w[..., IN_OFFSETS[2]:IN_OFFSETS[3]]
    forget = w[..., IN_OFFSETS[9]:IN_OFFSETS[10]]
    w_main = jnp.concatenate([w[..., :IN_OFFSETS[2]], w[..., IN_OFFSETS[3]:IN_OFFSETS[9]],
                              w[..., IN_OFFSETS[10]:]], axis=-1)
    w_tail = jnp.concatenate([k_rope, _rot_cols(k_rope), forget,
                              jnp.zeros((depth, D_MODEL, HEAD_DIM - N_HEADS), BF16)], axis=-1)
    uq = (w_uq * (MLA_QK_DIM ** -0.5 * LOG2E)).reshape(depth, Q_LORA, N_HEADS, MLA_QK_DIM)
    pe = uq[..., QK_NOPE:]
    uq = jnp.concatenate([uq[..., :QK_NOPE], pe, _rot_cols(pe)], axis=-1)
    uq = uq.reshape(depth, Q_LORA, N_HEADS * MLA_HEAD_PAD)
    ukv = w_ukv.reshape(depth, KV_LORA, N_HEADS, 2, HEAD_DIM).transpose(0, 1, 3, 2, 4)
    ukv = ukv.reshape(depth, KV_LORA, 2 * GROUP_WIDTH)
    bias_rows = jnp.pad(fox_forget_bias, ((0, 0), (0, HEAD_DIM - N_HEADS)))[:, None, :]
    return w_main, w_tail, uq.astype(BF16), ukv.astype(BF16), bias_rows


def _rope_tables(seq):
    pos = jnp.arange(seq, dtype=F32)[:, None]

    def angles(dim):
        inv_freq = ROPE_THETA ** (-jnp.arange(0, dim, 2, dtype=F32) / dim)
        return pos * inv_freq[None, :]

    ang_b = angles(HEAD_DIM)
    cos_b = jnp.concatenate([jnp.cos(ang_b)] * 2, axis=1)
    sin_b = jnp.concatenate([-jnp.sin(ang_b), jnp.sin(ang_b)], axis=1)
    ang_a = angles(QK_ROPE)
    zeros = jnp.zeros((seq, HEAD_DIM - QK_ROPE), F32)
    cos_a = jnp.concatenate([jnp.cos(ang_a)] * 2 + [zeros], axis=1)
    sin_a = jnp.concatenate([jnp.sin(ang_a)] * 2 + [zeros], axis=1)
    return cos_a, sin_a, cos_b, sin_b


def kernel(x, attn_norm, w_in, mla_q_norm, w_uq, mla_kv_norm, w_ukv, fox_forget_bias, group_norm,
           w_out, ffn_norm, w_gate, w_up, w_down, final_norm):
    batch, seq, _ = x.shape
    m = batch * seq
    assert seq % TQ_ATT == 0 and seq % (max(DILATIONS) * BAND) == 0 and m % TM_IN == 0
    w_main, w_tail, uq, ukv, bias_rows = _prep_weights(w_in, w_uq, w_ukv, fox_forget_bias)
    w_out_b, w_gate_b, w_up_b, w_down_b = (w.astype(BF16) for w in (w_out, w_gate, w_up, w_down))
    cos_a, sin_a, cos_b, sin_b = _rope_tables(seq)
    row = lambda v: v.reshape(1, -1)

    xf = x.reshape(m, D_MODEL)
    for l in range(DEPTH):
        proj, tail = _in_proj(xf, row(attn_norm[l]), w_main, w_tail, l)
        q_a, kv_a, kpe = _mla_prep(proj, tail, row(mla_q_norm[l]), row(mla_kv_norm[l]), uq, ukv, l,
                                   cos_a, sin_a, seq)
        ccol, crow = _fox_prep(tail, bias_rows[l], batch, seq)
        out_a = _mla_attention(q_a, kv_a, kpe, batch, seq)
        out_b = _dilated_attention(proj, cos_b, sin_b, batch, seq)
        out_c = _fox_attention(proj, ccol, crow, batch, seq)
        out_d = _sb_attention(proj, batch, seq)
        xf = _out_proj(out_a, out_b, out_c, out_d, row(group_norm[l]), w_out_b, l, xf)
        xf = _ffn(xf, row(ffn_norm[l]), w_gate_b, w_up_b, w_down_b, l, row(final_norm),
                  final=(l == DEPTH - 1))
    return xf.reshape(batch, seq, D_MODEL)
```

```python
import functools
import math

import jax
import jax.numpy as jnp
import numpy as np
from jax import lax
from jax.experimental import pallas as pl
from jax.experimental.pallas import tpu as pltpu

F32 = jnp.float32
BF16 = jnp.bfloat16

D_MODEL = 2048
DEPTH = 4
N_HEADS = 4
HEAD_DIM = 128
GROUP_WIDTH = N_HEADS * HEAD_DIM
Q_LORA = 512
KV_LORA = 512
QK_NOPE = 128
QK_ROPE = 64
MLA_QK_DIM = QK_NOPE + QK_ROPE
MLA_HEAD_PAD = 2 * HEAD_DIM
DILATED_PAIRS = ((128, 1), (512, 4), (2048, 16))
BAND = 128
assert all(window // dilation == BAND for window, dilation in DILATED_PAIRS)
DILATIONS = tuple(dilation for _, dilation in DILATED_PAIRS)
ROPE_THETA = 10000.0
FFN_HIDDEN = 5632
EPS = 1e-6
NEG_INF = -1e30
LOG2E = math.log2(math.e)

IN_SPLITS = ((Q_LORA, KV_LORA, QK_ROPE) + (GROUP_WIDTH,) * 3 + (GROUP_WIDTH,) * 3 + (N_HEADS,)
             + (GROUP_WIDTH,) * 3)
IN_OFFSETS = tuple(int(v) for v in np.cumsum((0,) + IN_SPLITS))

MAIN_WIDTH = Q_LORA + KV_LORA + 9 * GROUP_WIDTH
TAIL_WIDTH = 2 * HEAD_DIM
COL_QB, COL_KB, COL_VB = 8, 12, 16
COL_QC, COL_KC, COL_VC = 20, 24, 28
COL_QD, COL_KD, COL_VD = 32, 36, 40

VMEM_LIMIT_BYTES = 56 * 1024 * 1024
TM_IN, TN_IN = 512, 512
TM_MLA = 1024
TM_OUT = 512
ROWS_OUT = 128
TM_FFN, TF_FFN = 1024, 512
TQ_ATT, TK_ATT = 4096, 512
TQ_SB, T_SB = 1024, 256
SB_DEAD_LOG2 = -150.0
FOX_DEAD_LOG2 = -150.0
FOX_NEAR_CHUNKS = 3
TK_PREP = 256
ROWS_DIL = 512
CUM_CHUNK = 256


def _params(*sem):
    return pltpu.CompilerParams(dimension_semantics=sem, vmem_limit_bytes=VMEM_LIMIT_BYTES)


def _rms(x, gain):
    return (x * lax.rsqrt(jnp.mean(x * x, axis=-1, keepdims=True) + EPS)) * gain


def _dot(a, b):
    return jnp.dot(a, b, preferred_element_type=F32)


def _dot_nt(a, b):
    return lax.dot_general(a, b, (((1,), (1,)), ((), ())), preferred_element_type=F32)


def _lane_tile(x, n):
    return jnp.concatenate([x] * n, axis=1)


def _rope128(y, cos, sin):
    return y * cos + pltpu.roll(y, 64, 1) * sin


def _in_proj_kernel(x_ref, g_ref, w_ref, wt_ref, o_ref, ot_ref):
    hb = _rms(x_ref[...], g_ref[...]).astype(BF16)
    ot_ref[...] = _dot(hb, wt_ref[...])
    for j in range(MAIN_WIDTH // TN_IN):
        cols = slice(j * TN_IN, (j + 1) * TN_IN)
        o_ref[:, cols] = _dot(hb, w_ref[:, cols]).astype(o_ref.dtype)


def _in_proj(x, gain, w_main, w_tail, layer):
    m = x.shape[0]
    resident = dict(pipeline_mode=pl.Buffered(1))
    return pl.pallas_call(
        _in_proj_kernel,
        grid=(m // TM_IN,),
        in_specs=[
            pl.BlockSpec((TM_IN, D_MODEL), lambda i: (i, 0)),
            pl.BlockSpec((1, D_MODEL), lambda i: (0, 0)),
            pl.BlockSpec((None, D_MODEL, MAIN_WIDTH), lambda i: (layer, 0, 0), **resident),
            pl.BlockSpec((None, D_MODEL, TAIL_WIDTH), lambda i: (layer, 0, 0), **resident),
        ],
        out_specs=[
            pl.BlockSpec((TM_IN, MAIN_WIDTH), lambda i: (i, 0)),
            pl.BlockSpec((TM_IN, TAIL_WIDTH), lambda i: (i, 0)),
        ],
        out_shape=[
            jax.ShapeDtypeStruct((m, MAIN_WIDTH), BF16),
            jax.ShapeDtypeStruct((m, TAIL_WIDTH), F32),
        ],
        compiler_params=_params("parallel"),
        name="in_proj",
    )(x, gain, w_main, w_tail)


def _mla_prep_kernel(ql_ref, kvl_ref, t_ref, gq_ref, gkv_ref, wq_ref, wkv_ref, cos_ref, sin_ref,
                     q_ref, kv_ref, kpe_ref):
    cos = cos_ref[...]
    sin = sin_ref[...]
    hq = _rms(ql_ref[...].astype(F32), gq_ref[...]).astype(BF16)
    q = _dot(hq, wq_ref[...])
    for h in range(N_HEADS):
        lo = h * MLA_HEAD_PAD
        q_ref[:, lo:lo + HEAD_DIM] = q[:, lo:lo + HEAD_DIM].astype(BF16)
        y = q[:, lo + HEAD_DIM:lo + MLA_HEAD_PAD]
        q_ref[:, lo + HEAD_DIM:lo + MLA_HEAD_PAD] = _rope128(y, cos, sin).astype(BF16)
    hkv = _rms(kvl_ref[...].astype(F32), gkv_ref[...]).astype(BF16)
    kv_ref[...] = _dot(hkv, wkv_ref[...]).astype(BF16)
    kpe_ref[...] = _rope128(t_ref[...], cos, sin).astype(BF16)


def _mla_prep(proj, tail, gq, gkv, wq, wkv, layer, cos_t, sin_t, seq):
    m = proj.shape[0]
    nseq = seq // TM_MLA
    return pl.pallas_call(
        _mla_prep_kernel,
        grid=(m // TM_MLA,),
        in_specs=[
            pl.BlockSpec((TM_MLA, Q_LORA), lambda i: (i, 0)),
            pl.BlockSpec((TM_MLA, KV_LORA), lambda i: (i, 1)),
            pl.BlockSpec((TM_MLA, HEAD_DIM), lambda i: (i, 0)),
            pl.BlockSpec((1, Q_LORA), lambda i: (0, 0)),
            pl.BlockSpec((1, KV_LORA), lambda i: (0, 0)),
            pl.BlockSpec((None, Q_LORA, N_HEADS * MLA_HEAD_PAD), lambda i: (layer, 0, 0)),
            pl.BlockSpec((None, KV_LORA, 2 * GROUP_WIDTH), lambda i: (layer, 0, 0)),
            pl.BlockSpec((TM_MLA, HEAD_DIM), lambda i: (i % nseq, 0)),
            pl.BlockSpec((TM_MLA, HEAD_DIM), lambda i: (i % nseq, 0)),
        ],
        out_specs=[
            pl.BlockSpec((TM_MLA, N_HEADS * MLA_HEAD_PAD), lambda i: (i, 0)),
            pl.BlockSpec((TM_MLA, 2 * GROUP_WIDTH), lambda i: (i, 0)),
            pl.BlockSpec((TM_MLA, HEAD_DIM), lambda i: (i, 0)),
        ],
        out_shape=[
            jax.ShapeDtypeStruct((m, N_HEADS * MLA_HEAD_PAD), BF16),
            jax.ShapeDtypeStruct((m, 2 * GROUP_WIDTH), BF16),
            jax.ShapeDtypeStruct((m, HEAD_DIM), BF16),
        ],
        compiler_params=_params("parallel"),
        name="mla_prep",
    )(proj, proj, tail, gq, gkv, wq, wkv, cos_t, sin_t)


def _split3(x):
    hi = x.astype(BF16)
    r1 = x - hi.astype(F32)
    mid = r1.astype(BF16)
    lo = (r1 - mid.astype(F32)).astype(BF16)
    return hi, mid, lo


def _fox_prep_kernel(t_ref, b_ref, col_ref, row_ref, *, seq):
    r = lax.broadcasted_iota(jnp.int32, (CUM_CHUNK, CUM_CHUNK), 0)
    c = lax.broadcasted_iota(jnp.int32, (CUM_CHUNK, CUM_CHUNK), 1)
    lower = jnp.where(c <= r, 1.0, 0.0).astype(BF16)
    carry = jnp.zeros((1, HEAD_DIM), F32)
    for ci in range(seq // CUM_CHUNK):
        rows = slice(ci * CUM_CHUNK, (ci + 1) * CUM_CHUNK)
        x = t_ref[rows, :] + b_ref[...]
        log_f = jnp.minimum(x, 0.0) - jnp.log(1.0 + jnp.exp(-jnp.abs(x)))
        hi, mid, lo = _split3(log_f)
        cum = (_dot(lower, hi) + _dot(lower, mid)) + _dot(lower, lo) + carry
        col_ref[rows, :] = cum
        cum_t = cum.T
        for h in range(N_HEADS):
            row_ref[0, h, :, rows] = cum_t[h:h + 1, :]
        carry = cum[CUM_CHUNK - 1:CUM_CHUNK, :]


def _fox_prep(tail, bias_row, batch, seq):
    m = tail.shape[0]
    return pl.pallas_call(
        functools.partial(_fox_prep_kernel, seq=seq),
        grid=(batch,),
        in_specs=[
            pl.BlockSpec((seq, HEAD_DIM), lambda b: (b, 1)),
            pl.BlockSpec((1, HEAD_DIM), lambda b: (0, 0)),
        ],
        out_specs=[
            pl.BlockSpec((seq, HEAD_DIM), lambda b: (b, 0)),
            pl.BlockSpec((1, N_HEADS, 1, seq), lambda b: (b, 0, 0, 0)),
        ],
        out_shape=[
            jax.ShapeDtypeStruct((m, HEAD_DIM), F32),
            jax.ShapeDtypeStruct((batch, N_HEADS, 1, seq), F32),
        ],
        compiler_params=_params("parallel"),
        name="fox_prep",
    )(tail, bias_row)


def _mla_kernel(q_ref, k_ref, kpe_ref, v_ref, o_ref, m_s, acc_s):
    t = TK_ATT
    n_sub = TQ_ATT // t
    reps = t // HEAD_DIM
    qi = pl.program_id(2)
    ones = jnp.ones((t, HEAD_DIM), BF16)
    m_s[...] = jnp.full(m_s.shape, NEG_INF, F32)
    acc_s[...] = jnp.zeros(acc_s.shape, F32)
    causal = (lax.broadcasted_iota(jnp.int32, (t, t), 1) <= lax.broadcasted_iota(jnp.int32, (t, t), 0))

    def chunk(start, subs):
        k = jnp.concatenate([k_ref[pl.ds(start, t), :], kpe_ref[pl.ds(start, t), :]], axis=1)
        v_aug = jnp.concatenate([v_ref[pl.ds(start, t), :], ones], axis=1)
        rows = [slice(sub * t, (sub + 1) * t) for sub, _ in subs]
        scores = [_dot_nt(q_ref[r, :], k) for r in rows]
        probs, alphas = [], []
        for r, s, (_, diagonal) in zip(rows, scores, subs):
            if diagonal:
                s = jnp.where(causal, s, NEG_INF)
            m_prev = m_s[r, :]
            m_new = jnp.maximum(m_prev, jnp.max(s, axis=-1, keepdims=True))
            m_s[r, :] = m_new
            alphas.append(jnp.exp2(m_prev - m_new))
            probs.append(jnp.exp2(s - _lane_tile(m_new, reps)).astype(BF16))
        for r, p, alpha in zip(rows, probs, alphas):
            acc_s[r, :] = _lane_tile(alpha, 2) * acc_s[r, :] + _dot(p, v_aug)

    def body(c, carry):
        chunk(pl.multiple_of(c * t, t), tuple((sub, False) for sub in range(n_sub)))
        return carry

    lax.fori_loop(0, qi * n_sub, body, 0)
    base = qi * TQ_ATT
    for j in range(n_sub):
        chunk(pl.multiple_of(base + j * t, t), tuple((sub, sub == j) for sub in range(j, n_sub)))
    acc = acc_s[...]
    o_ref[...] = (acc[:, :HEAD_DIM] / acc[:, HEAD_DIM:]).astype(o_ref.dtype)


def _mla_attention(q_a, kv_a, kpe, batch, seq):
    m = q_a.shape[0]
    nq = seq // TQ_ATT
    return pl.pallas_call(
        _mla_kernel,
        grid=(batch, N_HEADS, nq),
        in_specs=[
            pl.BlockSpec((TQ_ATT, MLA_HEAD_PAD), lambda b, h, i: (b * nq + i, h)),
            pl.BlockSpec((seq, HEAD_DIM), lambda b, h, i: (b, h)),
            pl.BlockSpec((seq, HEAD_DIM), lambda b, h, i: (b, 0)),
            pl.BlockSpec((seq, HEAD_DIM), lambda b, h, i: (b, N_HEADS + h)),
        ],
        out_specs=pl.BlockSpec((TQ_ATT, HEAD_DIM), lambda b, h, i: (b * nq + i, h)),
        out_shape=jax.ShapeDtypeStruct((m, GROUP_WIDTH), BF16),
        scratch_shapes=[pltpu.VMEM((TQ_ATT, HEAD_DIM), F32), pltpu.VMEM((TQ_ATT, 2 * HEAD_DIM), F32)],
        compiler_params=_params("parallel", "parallel", "arbitrary"),
        name="mla_attention",
    )(q_a, kv_a, kpe, kv_a)


def _fox_kernel(q_ref, k_ref, v_ref, ccol_ref, crow_ref, o_ref, m_s, acc_s, ct_s, qn_s, *, seq):
    t = TK_ATT
    n_sub = seq // t
    reps = t // HEAD_DIM
    head = pl.program_id(1)
    ones = jnp.ones((t, HEAD_DIM), BF16)
    causal = (lax.broadcasted_iota(jnp.int32, (t, t), 1) <= lax.broadcasted_iota(jnp.int32, (t, t), 0))
    m_s[...] = jnp.full(m_s.shape, NEG_INF, F32)
    acc_s[...] = jnp.zeros(acc_s.shape, F32)
    lane = lax.broadcasted_iota(jnp.int32, (t, HEAD_DIM), 1)
    k_sq_max = jnp.zeros((t, 1), F32)
    for sub in range(n_sub):
        rows = slice(sub * t, (sub + 1) * t)
        c_t = jnp.sum(jnp.where(lane == head, ccol_ref[rows, :], 0.0), axis=1, keepdims=True)
        ct_s[rows, :] = jnp.broadcast_to(c_t * LOG2E, (t, HEAD_DIM))
        q = q_ref[rows, :].astype(F32)
        qn_s[rows, :] = jnp.broadcast_to(jnp.sum(q * q, axis=1, keepdims=True), (t, HEAD_DIM))
        k = k_ref[rows, :].astype(F32)
        k_sq_max = jnp.maximum(k_sq_max, jnp.sum(k * k, axis=1, keepdims=True))
    k_sq = jnp.max(k_sq_max)

    def attend(dist, diagonal):
        static = isinstance(dist, int)
        subs = [sub for sub in range(n_sub) if not static or sub - dist >= 0]
        rows = [slice(sub * t, (sub + 1) * t) for sub in subs]
        chunk_ids = [sub - dist for sub in subs]
        starts = [c * t if static else pl.multiple_of(jnp.maximum(c, 0) * t, t) for c in chunk_ids]
        scores = [_dot_nt(q_ref[r, :], k_ref[pl.ds(st, t), :]) for r, st in zip(rows, starts)]
        probs, alphas = [], []
        for r, s, c, st in zip(rows, scores, chunk_ids, starts):
            s = s - crow_ref[0, 0, :, pl.ds(st, t)] * LOG2E
            if diagonal:
                s = jnp.where(causal, s, NEG_INF)
            if not static:
                s = jnp.where(c >= 0, s, NEG_INF)
            m_prev = m_s[r, :]
            m_new = jnp.maximum(m_prev, jnp.max(s, axis=-1, keepdims=True) + ct_s[r, :])
            m_s[r, :] = m_new
            alphas.append(jnp.exp2(m_prev - m_new))
            probs.append(jnp.exp2(s - _lane_tile(m_new - ct_s[r, :], reps)).astype(BF16))
        for r, p, alpha, st in zip(rows, probs, alphas, starts):
            v_aug = jnp.concatenate([v_ref[pl.ds(st, t), :], ones], axis=1)
            acc_s[r, :] = _lane_tile(alpha, 2) * acc_s[r, :] + _dot(p, v_aug)

    def live_top(dist):
        tops = []
        for sub in range(n_sub):
            rows = slice(sub * t, (sub + 1) * t)
            c = sub - dist
            st = pl.multiple_of(jnp.maximum(c, 0) * t, t)
            c_s_min = jnp.min(crow_ref[0, 0, :, pl.ds(st, t)]) * LOG2E
            slack = jnp.max(jnp.sqrt(qn_s[rows, :] * k_sq) + ct_s[rows, :] - m_s[rows, :])
            tops.append(jnp.where(c >= 0, slack - c_s_min, FOX_DEAD_LOG2))
        return functools.reduce(jnp.maximum, tops)

    attend(0, True)
    for dist in range(1, min(FOX_NEAR_CHUNKS, n_sub)):
        attend(dist, False)

    def more(carry):
        return carry[1] > FOX_DEAD_LOG2

    def body(carry):
        dist = carry[0]
        attend(dist, False)
        return dist + 1, live_top(dist + 1)

    lax.while_loop(more, body, (jnp.int32(FOX_NEAR_CHUNKS), live_top(FOX_NEAR_CHUNKS)))
    acc = acc_s[...]
    o_ref[...] = (acc[:, :HEAD_DIM] / acc[:, HEAD_DIM:]).astype(o_ref.dtype)


def _fox_attention(proj, ccol, crow, batch, seq):
    m = proj.shape[0]
    blk = (seq, HEAD_DIM)
    stat = pltpu.VMEM(blk, F32)
    return pl.pallas_call(
        functools.partial(_fox_kernel, seq=seq),
        grid=(batch, N_HEADS),
        in_specs=[
            pl.BlockSpec(blk, lambda b, h: (b, COL_QC + h)),
            pl.BlockSpec(blk, lambda b, h: (b, COL_KC + h)),
            pl.BlockSpec(blk, lambda b, h: (b, COL_VC + h)),
            pl.BlockSpec(blk, lambda b, h: (b, 0)),
            pl.BlockSpec((1, 1, 1, seq), lambda b, h: (b, h, 0, 0)),
        ],
        out_specs=pl.BlockSpec(blk, lambda b, h: (b, h)),
        out_shape=jax.ShapeDtypeStruct((m, GROUP_WIDTH), BF16),
        scratch_shapes=[stat, pltpu.VMEM((seq, 2 * HEAD_DIM), F32), stat, stat],
        compiler_params=_params("parallel", "parallel"),
        name="fox_attention",
    )(proj, proj, proj, ccol, crow)


def _sb_kernel(q_ref, k_ref, v_ref, o_ref, r_s, acc_s):
    t = T_SB
    n_sub = TQ_SB // t
    reps = t // HEAD_DIM
    first = pl.program_id(2) * n_sub
    row = lax.broadcasted_iota(jnp.int32, (t, t), 0)
    col = lax.broadcasted_iota(jnp.int32, (t, t), 1)
    later = jnp.where(row > col, 1.0, 0.0).astype(BF16)
    past = col < row
    r_s[...] = jnp.zeros(r_s.shape, F32)
    acc_s[...] = jnp.zeros(acc_s.shape, F32)

    def attend(dist, diagonal):
        rows = [slice(sub * t, (sub + 1) * t) for sub in range(n_sub)]
        chunk_ids = [first + (sub - dist) for sub in range(n_sub)]
        always = [isinstance(dist, int) and sub - dist >= 0 for sub in range(n_sub)]
        starts = [pl.multiple_of(jnp.maximum(c, 0) * t, t) for c in chunk_ids]
        logits = [_dot_nt(q_ref[r, :], k_ref[pl.ds(st, t), :]) for r, st in zip(rows, starts)]
        log_betas, log_keeps = [], []
        for z, c, sure in zip(logits, chunk_ids, always):
            neg_abs = pltpu.bitcast(pltpu.bitcast(z, jnp.uint32) | jnp.uint32(0x80000000), F32)
            soft = jnp.log2(1.0 + jnp.exp2(neg_abs))
            log_beta = jnp.minimum(z, 0.0) - soft
            log_keep = log_beta - z
            if diagonal:
                log_keep = jnp.where(past, log_keep, 0.0)
            if not sure:
                log_keep = jnp.where(c >= 0, log_keep, 0.0)
            log_betas.append(log_beta)
            log_keeps.append(log_keep)
        betweens = [_dot(lk.astype(BF16), later) for lk in log_keeps]
        weights = []
        for r, log_beta, between, c, sure in zip(rows, log_betas, betweens, chunk_ids, always):
            a = jnp.exp2(log_beta + (between + _lane_tile(r_s[r, :], reps)))
            if diagonal:
                a = jnp.where(past, a, 0.0)
            if not sure:
                a = jnp.where(c >= 0, a, 0.0)
            weights.append(a.astype(BF16))
        for r, a, log_keep, st in zip(rows, weights, log_keeps, starts):
            acc_s[r, :] += _dot(a, v_ref[pl.ds(st, t), :])
            r_s[r, :] += jnp.sum(log_keep, axis=-1, keepdims=True)

    def live_max(dist):
        tops = [jnp.where(first + sub - dist >= 0, jnp.max(r_s[sub * t:(sub + 1) * t, :]), SB_DEAD_LOG2)
                for sub in range(n_sub)]
        return functools.reduce(jnp.maximum, tops)

    attend(0, True)
    attend(1, False)

    def more(carry):
        return carry[1] > SB_DEAD_LOG2

    def body(carry):
        dist = carry[0]
        attend(dist, False)
        return dist + 1, live_max(dist + 1)

    lax.while_loop(more, body, (jnp.int32(2), live_max(2)))
    o_ref[...] = acc_s[...].astype(o_ref.dtype)


def _sb_attention(proj, batch, seq):
    m = proj.shape[0]
    nq = seq // TQ_SB
    return pl.pallas_call(
        _sb_kernel,
        grid=(batch, N_HEADS, nq),
        in_specs=[
            pl.BlockSpec((TQ_SB, HEAD_DIM), lambda b, h, i: (b * nq + i, COL_QD + h)),
            pl.BlockSpec((seq, HEAD_DIM), lambda b, h, i: (b, COL_KD + h)),
            pl.BlockSpec((seq, HEAD_DIM), lambda b, h, i: (b, COL_VD + h)),
        ],
        out_specs=pl.BlockSpec((TQ_SB, HEAD_DIM), lambda b, h, i: (b * nq + i, h)),
        out_shape=jax.ShapeDtypeStruct((m, GROUP_WIDTH), BF16),
        scratch_shapes=[pltpu.VMEM((TQ_SB, HEAD_DIM), F32), pltpu.VMEM((TQ_SB, HEAD_DIM), F32)],
        compiler_params=_params("parallel", "parallel", "arbitrary"),
        name="sb_attention",
    )(proj, proj, proj)


def _dilated_kernel(q_ref, k_ref, v_ref, cos_ref, sin_ref, o_ref, q_s, k_s, v_s, og_s, lg_s, *, seq):
    for c in range(seq // ROWS_DIL):
        rows = slice(c * ROWS_DIL, (c + 1) * ROWS_DIL)
        cos = cos_ref[rows, :]
        sin = sin_ref[rows, :]
        q_s[rows, :] = _rope128(q_ref[rows, :].astype(F32), cos, sin)
        k_s[rows, :] = _rope128(k_ref[rows, :].astype(F32), cos, sin)
        v_s[rows, :] = v_ref[rows, :].astype(F32)

    ones = jnp.ones((2 * BAND, HEAD_DIM), BF16)
    row1 = lax.broadcasted_iota(jnp.int32, (BAND, BAND), 0)
    col1 = lax.broadcasted_iota(jnp.int32, (BAND, BAND), 1)
    valid1 = col1 <= row1
    row2 = lax.broadcasted_iota(jnp.int32, (BAND, 2 * BAND), 0)
    col2 = lax.broadcasted_iota(jnp.int32, (BAND, 2 * BAND), 1)
    dist = BAND + row2 - col2
    valid2 = (dist >= 0) & (dist <= BAND)

    def attend(g, d, q_start, k_start, n_keys, valid):
        def strided(start, n):
            return pl.ds(start, n, stride=d) if d > 1 else pl.ds(start, n)
        qb = q_s[strided(q_start, BAND), :].astype(BF16)
        kb = k_s[strided(k_start, n_keys), :].astype(BF16)
        vb = v_s[strided(k_start, n_keys), :].astype(BF16)
        s = jnp.where(valid, _dot_nt(qb, kb), NEG_INF)
        m = jnp.max(s, axis=-1, keepdims=True)
        p = jnp.exp2(s - m)
        pv = _dot(p.astype(BF16), jnp.concatenate([vb, ones[:n_keys]], axis=1))
        den = pv[:, HEAD_DIM:]
        og_s[g, strided(q_start, BAND), :] = pv[:, :HEAD_DIM] / den
        lg_s[g, strided(q_start, BAND), :] = m + jnp.log2(den)

    for g, d in enumerate(DILATIONS):
        nb = seq // d // BAND

        def residue(r, carry, g=g, d=d, nb=nb):
            attend(g, d, r, r, BAND, valid1)

            def block(i, c2):
                attend(g, d, r + i * (BAND * d), r + (i - 1) * (BAND * d), 2 * BAND, valid2)
                return c2

            lax.fori_loop(1, nb, block, 0, unroll=4 if nb > 2 else 1)
            return carry

        lax.fori_loop(0, d, residue, 0, unroll=4 if (nb <= 2 and d > 1) else 1)

    for c in range(seq // ROWS_DIL):
        rows = slice(c * ROWS_DIL, (c + 1) * ROWS_DIL)
        l1, l2, l3 = lg_s[0, rows, :], lg_s[1, rows, :], lg_s[2, rows, :]
        top = jnp.maximum(jnp.maximum(l1, l2), l3)
        e1, e2, e3 = jnp.exp2(l1 - top), jnp.exp2(l2 - top), jnp.exp2(l3 - top)
        num = (e1 * og_s[0, rows, :] + e2 * og_s[1, rows, :]) + e3 * og_s[2, rows, :]
        o_ref[rows, :] = (num / ((e1 + e2) + e3)).astype(o_ref.dtype)


def _dilated_attention(proj, cos_b, sin_b, batch, seq):
    m = proj.shape[0]
    blk = (seq, HEAD_DIM)
    n_groups = len(DILATIONS)
    return pl.pallas_call(
        functools.partial(_dilated_kernel, seq=seq),
        grid=(batch, N_HEADS),
        in_specs=[
            pl.BlockSpec(blk, lambda b, h: (b, COL_QB + h)),
            pl.BlockSpec(blk, lambda b, h: (b, COL_KB + h)),
            pl.BlockSpec(blk, lambda b, h: (b, COL_VB + h)),
            pl.BlockSpec(blk, lambda b, h: (0, 0)),
            pl.BlockSpec(blk, lambda b, h: (0, 0)),
        ],
        out_specs=pl.BlockSpec(blk, lambda b, h: (b, h)),
        out_shape=jax.ShapeDtypeStruct((m, GROUP_WIDTH), BF16),
        scratch_shapes=[pltpu.VMEM(blk, F32), pltpu.VMEM(blk, F32), pltpu.VMEM(blk, F32),
                        pltpu.VMEM((n_groups,) + blk, F32), pltpu.VMEM((n_groups,) + blk, F32)],
        compiler_params=_params("parallel", "parallel"),
        name="dilated_attention",
    )(proj, proj, proj, cos_b, sin_b)


def _out_proj_kernel(oa_ref, ob_ref, oc_ref, od_ref, gn_ref, w_ref, x_ref, o_ref):
    for sub in range(TM_OUT // ROWS_OUT):
        rows = slice(sub * ROWS_OUT, (sub + 1) * ROWS_OUT)
        parts = []
        for group, ref in enumerate((oa_ref, ob_ref, oc_ref, od_ref)):
            cols = slice(group * GROUP_WIDTH, (group + 1) * GROUP_WIDTH)
            parts.append(_rms(ref[rows, :].astype(F32), gn_ref[:, cols]).astype(BF16))
        g = jnp.concatenate(parts, axis=1)
        o_ref[rows, :] = x_ref[rows, :] + _dot(g, w_ref[...])


def _out_proj(oa, ob, oc, od, gn, w_out, layer, x):
    m = x.shape[0]
    grp = pl.BlockSpec((TM_OUT, GROUP_WIDTH), lambda i: (i, 0))
    return pl.pallas_call(
        _out_proj_kernel,
        grid=(m // TM_OUT,),
        in_specs=[
            grp, grp, grp, grp,
            pl.BlockSpec((1, D_MODEL), lambda i: (0, 0)),
            pl.BlockSpec((None, D_MODEL, D_MODEL), lambda i: (layer, 0, 0)),
            pl.BlockSpec((TM_OUT, D_MODEL), lambda i: (i, 0)),
        ],
        out_specs=pl.BlockSpec((TM_OUT, D_MODEL), lambda i: (i, 0)),
        out_shape=jax.ShapeDtypeStruct((m, D_MODEL), F32),
        compiler_params=_params("parallel"),
        name="out_proj",
    )(oa, ob, oc, od, gn, w_out, x)


def _ffn_kernel(x_ref, g_ref, wg_ref, wu_ref, wd_ref, fg_ref, o_ref, h_s, *, final):
    f = pl.program_id(1)

    @pl.when(f == 0)
    def _():
        x = x_ref[...]
        h_s[...] = _rms(x, g_ref[...]).astype(BF16)
        o_ref[...] = x

    h = h_s[...]
    gate = _dot(h, wg_ref[...])
    up = _dot(h, wu_ref[...])
    act = (gate / (1.0 + jnp.exp(-gate))) * up
    o_ref[...] += _dot(act.astype(BF16), wd_ref[...])

    if final:
        @pl.when(f == pl.num_programs(1) - 1)
        def _():
            o_ref[...] = _rms(o_ref[...], fg_ref[...])


def _ffn(x, gain, w_gate, w_up, w_down, layer, final_gain, final):
    m = x.shape[0]
    return pl.pallas_call(
        functools.partial(_ffn_kernel, final=final),
        grid=(m // TM_FFN, FFN_HIDDEN // TF_FFN),
        in_specs=[
            pl.BlockSpec((TM_FFN, D_MODEL), lambda i, f: (i, 0)),
            pl.BlockSpec((1, D_MODEL), lambda i, f: (0, 0)),
            pl.BlockSpec((None, D_MODEL, TF_FFN), lambda i, f: (layer, 0, f)),
            pl.BlockSpec((None, D_MODEL, TF_FFN), lambda i, f: (layer, 0, f)),
            pl.BlockSpec((None, TF_FFN, D_MODEL), lambda i, f: (layer, f, 0)),
            pl.BlockSpec((1, D_MODEL), lambda i, f: (0, 0)),
        ],
        out_specs=pl.BlockSpec((TM_FFN, D_MODEL), lambda i, f: (i, 0)),
        out_shape=jax.ShapeDtypeStruct((m, D_MODEL), F32),
        scratch_shapes=[pltpu.VMEM((TM_FFN, D_MODEL), BF16)],
        compiler_params=_params("parallel", "arbitrary"),
        name="ffn_final" if final else "ffn",
    )(x, gain, w_gate, w_up, w_down, final_gain)


def _rot_cols(w):
    half = w.shape[-1] // 2
    return jnp.concatenate([-w[..., half:], w[..., :half]], axis=-1)


def _w_in_kernel(w_ref, s_ref, main_ref, tail_ref):
    w = w_ref[...] * s_ref[...]
    o = IN_OFFSETS
    main_ref[...] = jnp.concatenate([w[:, :o[2]], w[:, o[3]:o[9]], w[:, o[10]:]], axis=1).astype(BF16)
    k_rope = w[:, o[2]:o[3]]
    pad = jnp.zeros((w.shape[0], HEAD_DIM - N_HEADS), F32)
    tail_ref[...] = jnp.concatenate([k_rope, _rot_cols(k_rope), w[:, o[9]:o[10]], pad], axis=1).astype(BF16)


def _prep_w_in(w_in, col_scale):
    depth, _, width = w_in.shape
    return pl.pallas_call(
        _w_in_kernel,
        grid=(depth, D_MODEL // TK_PREP),
        in_specs=[
            pl.BlockSpec((None, TK_PREP, width), lambda l, i: (l, i, 0)),
            pl.BlockSpec((1, width), lambda l, i: (0, 0)),
        ],
        out_specs=[
            pl.BlockSpec((None, TK_PREP, MAIN_WIDTH), lambda l, i: (l, i, 0)),
            pl.BlockSpec((None, TK_PREP, TAIL_WIDTH), lambda l, i: (l, i, 0)),
        ],
        out_shape=[
            jax.ShapeDtypeStruct((depth, D_MODEL, MAIN_WIDTH), BF16),
            jax.ShapeDtypeStruct((depth, D_MODEL, TAIL_WIDTH), BF16),
        ],
        compiler_params=_params("parallel", "parallel"),
        name="prep_w_in",
    )(w_in, col_scale)


def _prep_weights(w_in, w_uq, w_ukv, fox_forget_bias):
    depth = w_in.shape[0]
    col_scale = np.ones((1, IN_OFFSETS[-1]), np.float32)
    for q_split in (3, 6, 10):
        col_scale[:, IN_OFFSETS[q_split]:IN_OFFSETS[q_split + 1]] = HEAD_DIM ** -0.5 * LOG2E
    w_main, w_tail = _prep_w_in(w_in, jnp.asarray(col_scale))
    uq = (w_uq * (MLA_QK_DIM ** -0.5 * LOG2E)).reshape(depth, Q_LORA, N_HEADS, MLA_QK_DIM)
    pe = uq[..., QK_NOPE:]
    uq = jnp.concatenate([uq[..., :QK_NOPE], pe, _rot_cols(pe)], axis=-1)
    uq = uq.reshape(depth, Q_LORA, N_HEADS * MLA_HEAD_PAD)
    ukv = w_ukv.reshape(depth, KV_LORA, N_HEADS, 2, HEAD_DIM).transpose(0, 1, 3, 2, 4)
    ukv = ukv.reshape(depth, KV_LORA, 2 * GROUP_WIDTH)
    bias_rows = jnp.pad(fox_forget_bias, ((0, 0), (0, HEAD_DIM - N_HEADS)))[:, None, :]
    return w_main, w_tail, uq.astype(BF16), ukv.astype(BF16), bias_rows


def _rope_tables(seq):
    pos = jnp.arange(seq, dtype=F32)[:, None]

    def angles(dim):
        inv_freq = ROPE_THETA ** (-jnp.arange(0, dim, 2, dtype=F32) / dim)
        return pos * inv_freq[None, :]

    ang_b = angles(HEAD_DIM)
    cos_b = jnp.concatenate([jnp.cos(ang_b)] * 2, axis=1)
    sin_b = jnp.concatenate([-jnp.sin(ang_b), jnp.sin(ang_b)], axis=1)
    ang_a = angles(QK_ROPE)
    zeros = jnp.zeros((seq, HEAD_DIM - QK_ROPE), F32)
    cos_a = jnp.concatenate([jnp.cos(ang_a)] * 2 + [zeros], axis=1)
    sin_a = jnp.concatenate([jnp.sin(ang_a)] * 2 + [zeros], axis=1)
    return cos_a, sin_a, cos_b, sin_b


def kernel(x, attn_norm, w_in, mla_q_norm, w_uq, mla_kv_norm, w_ukv, fox_forget_bias, group_norm,
           w_out, ffn_norm, w_gate, w_up, w_down, final_norm):
    batch, seq, _ = x.shape
    m = batch * seq
    assert seq % TQ_ATT == 0 and seq % (max(DILATIONS) * BAND) == 0 and m % TM_IN == 0
    w_main, w_tail, uq, ukv, bias_rows = _prep_weights(w_in, w_uq, w_ukv, fox_forget_bias)
    w_out_b, w_gate_b, w_up_b, w_down_b = (w.astype(BF16) for w in (w_out, w_gate, w_up, w_down))
    cos_a, sin_a, cos_b, sin_b = _rope_tables(seq)
    row = lambda v: v.reshape(1, -1)

    xf = x.reshape(m, D_MODEL)
    for l in range(DEPTH):
        proj, tail = _in_proj(xf, row(attn_norm[l]), w_main, w_tail, l)
        q_a, kv_a, kpe = _mla_prep(proj, tail, row(mla_q_norm[l]), row(mla_kv_norm[l]), uq, ukv, l,
                                   cos_a, sin_a, seq)
        ccol, crow = _fox_prep(tail, bias_rows[l], batch, seq)
        out_a = _mla_attention(q_a, kv_a, kpe, batch, seq)
        out_b = _dilated_attention(proj, cos_b, sin_b, batch, seq)
        out_c = _fox_attention(proj, ccol, crow, batch, seq)
        out_d = _sb_attention(proj, batch, seq)
        xf = _out_proj(out_a, out_b, out_c, out_d, row(group_norm[l]), w_out_b, l, xf)
        xf = _ffn(xf, row(ffn_norm[l]), w_gate_b, w_up_b, w_down_b, l, row(final_norm),
                  final=(l == DEPTH - 1))
    return xf.reshape(batch, seq, D_MODEL)
```

```python
import functools
import math

import jax
import jax.numpy as jnp
import numpy as np
from jax import lax
from jax.experimental import pallas as pl
from jax.experimental.pallas import tpu as pltpu

F32 = jnp.float32
BF16 = jnp.bfloat16

D_MODEL = 2048
DEPTH = 4
N_HEADS = 4
HEAD_DIM = 128
GROUP_WIDTH = N_HEADS * HEAD_DIM
Q_LORA = 512
KV_LORA = 512
QK_NOPE = 128
QK_ROPE = 64
MLA_QK_DIM = QK_NOPE + QK_ROPE
MLA_HEAD_PAD = 2 * HEAD_DIM
DILATED_PAIRS = ((128, 1), (512, 4), (2048, 16))
BAND = 128
assert all(window // dilation == BAND for window, dilation in DILATED_PAIRS)
DILATIONS = tuple(dilation for _, dilation in DILATED_PAIRS)
ROPE_THETA = 10000.0
FFN_HIDDEN = 5632
EPS = 1e-6
NEG_INF = -1e30
LOG2E = math.log2(math.e)

IN_SPLITS = ((Q_LORA, KV_LORA, QK_ROPE) + (GROUP_WIDTH,) * 3 + (GROUP_WIDTH,) * 3 + (N_HEADS,)
             + (GROUP_WIDTH,) * 3)
IN_OFFSETS = tuple(int(v) for v in np.cumsum((0,) + IN_SPLITS))

MAIN_WIDTH = Q_LORA + KV_LORA + 9 * GROUP_WIDTH
TAIL_WIDTH = 2 * HEAD_DIM
COL_QB, COL_KB, COL_VB = 8, 12, 16
COL_QC, COL_KC, COL_VC = 20, 24, 28
COL_QD, COL_KD, COL_VD = 32, 36, 40

VMEM_LIMIT_BYTES = 56 * 1024 * 1024
TM_IN, TN_IN = 512, 512
TM_MLA = 1024
TM_OUT = 512
ROWS_OUT = 128
TM_FFN, TF_FFN = 1024, 512
TQ_ATT, TK_ATT = 4096, 512
TQ_SB, T_SB = 1024, 256
SB_DEAD_LOG2 = -150.0
FOX_DEAD_LOG2 = -150.0
FOX_NEAR_CHUNKS = 3
TK_PREP = 256
ROWS_DIL = 512
DIL_UNROLL = 8
CUM_CHUNK = 256


def _params(*sem):
    return pltpu.CompilerParams(dimension_semantics=sem, vmem_limit_bytes=VMEM_LIMIT_BYTES)


def _rms(x, gain):
    return (x * lax.rsqrt(jnp.mean(x * x, axis=-1, keepdims=True) + EPS)) * gain


def _dot(a, b):
    return jnp.dot(a, b, preferred_element_type=F32)


def _dot_nt(a, b):
    return lax.dot_general(a, b, (((1,), (1,)), ((), ())), preferred_element_type=F32)


def _lane_tile(x, n):
    return jnp.concatenate([x] * n, axis=1)


def _rope128(y, cos, sin):
    return y * cos + pltpu.roll(y, 64, 1) * sin


def _in_proj_kernel(x_ref, g_ref, w_ref, wt_ref, o_ref, ot_ref):
    hb = _rms(x_ref[...], g_ref[...]).astype(BF16)
    ot_ref[...] = _dot(hb, wt_ref[...])
    for j in range(MAIN_WIDTH // TN_IN):
        cols = slice(j * TN_IN, (j + 1) * TN_IN)
        o_ref[:, cols] = _dot(hb, w_ref[:, cols]).astype(o_ref.dtype)


def _in_proj(x, gain, w_main, w_tail, layer):
    m = x.shape[0]
    resident = dict(pipeline_mode=pl.Buffered(1))
    return pl.pallas_call(
        _in_proj_kernel,
        grid=(m // TM_IN,),
        in_specs=[
            pl.BlockSpec((TM_IN, D_MODEL), lambda i: (i, 0)),
            pl.BlockSpec((1, D_MODEL), lambda i: (0, 0)),
            pl.BlockSpec((None, D_MODEL, MAIN_WIDTH), lambda i: (layer, 0, 0), **resident),
            pl.BlockSpec((None, D_MODEL, TAIL_WIDTH), lambda i: (layer, 0, 0), **resident),
        ],
        out_specs=[
            pl.BlockSpec((TM_IN, MAIN_WIDTH), lambda i: (i, 0)),
            pl.BlockSpec((TM_IN, TAIL_WIDTH), lambda i: (i, 0)),
        ],
        out_shape=[
            jax.ShapeDtypeStruct((m, MAIN_WIDTH), BF16),
            jax.ShapeDtypeStruct((m, TAIL_WIDTH), F32),
        ],
        compiler_params=_params("parallel"),
        name="in_proj",
    )(x, gain, w_main, w_tail)


def _mla_prep_kernel(ql_ref, kvl_ref, t_ref, gq_ref, gkv_ref, wq_ref, wkv_ref, cos_ref, sin_ref,
                     q_ref, kv_ref, kpe_ref):
    cos = cos_ref[...]
    sin = sin_ref[...]
    hq = _rms(ql_ref[...].astype(F32), gq_ref[...]).astype(BF16)
    q = _dot(hq, wq_ref[...])
    for h in range(N_HEADS):
        lo = h * MLA_HEAD_PAD
        q_ref[:, lo:lo + HEAD_DIM] = q[:, lo:lo + HEAD_DIM].astype(BF16)
        y = q[:, lo + HEAD_DIM:lo + MLA_HEAD_PAD]
        q_ref[:, lo + HEAD_DIM:lo + MLA_HEAD_PAD] = _rope128(y, cos, sin).astype(BF16)
    hkv = _rms(kvl_ref[...].astype(F32), gkv_ref[...]).astype(BF16)
    kv_ref[...] = _dot(hkv, wkv_ref[...]).astype(BF16)
    kpe_ref[...] = _rope128(t_ref[...], cos, sin).astype(BF16)


def _mla_prep(proj, tail, gq, gkv, wq, wkv, layer, cos_t, sin_t, seq):
    m = proj.shape[0]
    nseq = seq // TM_MLA
    return pl.pallas_call(
        _mla_prep_kernel,
        grid=(m // TM_MLA,),
        in_specs=[
            pl.BlockSpec((TM_MLA, Q_LORA), lambda i: (i, 0)),
            pl.BlockSpec((TM_MLA, KV_LORA), lambda i: (i, 1)),
            pl.BlockSpec((TM_MLA, HEAD_DIM), lambda i: (i, 0)),
            pl.BlockSpec((1, Q_LORA), lambda i: (0, 0)),
            pl.BlockSpec((1, KV_LORA), lambda i: (0, 0)),
            pl.BlockSpec((None, Q_LORA, N_HEADS * MLA_HEAD_PAD), lambda i: (layer, 0, 0)),
            pl.BlockSpec((None, KV_LORA, 2 * GROUP_WIDTH), lambda i: (layer, 0, 0)),
            pl.BlockSpec((TM_MLA, HEAD_DIM), lambda i: (i % nseq, 0)),
            pl.BlockSpec((TM_MLA, HEAD_DIM), lambda i: (i % nseq, 0)),
        ],
        out_specs=[
            pl.BlockSpec((TM_MLA, N_HEADS * MLA_HEAD_PAD), lambda i: (i, 0)),
            pl.BlockSpec((TM_MLA, 2 * GROUP_WIDTH), lambda i: (i, 0)),
            pl.BlockSpec((TM_MLA, HEAD_DIM), lambda i: (i, 0)),
        ],
        out_shape=[
            jax.ShapeDtypeStruct((m, N_HEADS * MLA_HEAD_PAD), BF16),
            jax.ShapeDtypeStruct((m, 2 * GROUP_WIDTH), BF16),
            jax.ShapeDtypeStruct((m, HEAD_DIM), BF16),
        ],
        compiler_params=_params("parallel"),
        name="mla_prep",
    )(proj, proj, tail, gq, gkv, wq, wkv, cos_t, sin_t)


def _split3(x):
    hi = x.astype(BF16)
    r1 = x - hi.astype(F32)
    mid = r1.astype(BF16)
    lo = (r1 - mid.astype(F32)).astype(BF16)
    return hi, mid, lo


def _fox_prep_kernel(t_ref, b_ref, col_ref, row_ref, *, seq):
    r = lax.broadcasted_iota(jnp.int32, (CUM_CHUNK, CUM_CHUNK), 0)
    c = lax.broadcasted_iota(jnp.int32, (CUM_CHUNK, CUM_CHUNK), 1)
    lower = jnp.where(c <= r, 1.0, 0.0).astype(BF16)
    carry = jnp.zeros((1, HEAD_DIM), F32)
    for ci in range(seq // CUM_CHUNK):
        rows = slice(ci * CUM_CHUNK, (ci + 1) * CUM_CHUNK)
        x = t_ref[rows, :] + b_ref[...]
        log_f = jnp.minimum(x, 0.0) - jnp.log(1.0 + jnp.exp(-jnp.abs(x)))
        hi, mid, lo = _split3(log_f)
        cum = (_dot(lower, hi) + _dot(lower, mid)) + _dot(lower, lo) + carry
        col_ref[rows, :] = cum
        cum_t = cum.T
        for h in range(N_HEADS):
            row_ref[0, h, :, rows] = cum_t[h:h + 1, :]
        carry = cum[CUM_CHUNK - 1:CUM_CHUNK, :]


def _fox_prep(tail, bias_row, batch, seq):
    m = tail.shape[0]
    return pl.pallas_call(
        functools.partial(_fox_prep_kernel, seq=seq),
        grid=(batch,),
        in_specs=[
            pl.BlockSpec((seq, HEAD_DIM), lambda b: (b, 1)),
            pl.BlockSpec((1, HEAD_DIM), lambda b: (0, 0)),
        ],
        out_specs=[
            pl.BlockSpec((seq, HEAD_DIM), lambda b: (b, 0)),
            pl.BlockSpec((1, N_HEADS, 1, seq), lambda b: (b, 0, 0, 0)),
        ],
        out_shape=[
            jax.ShapeDtypeStruct((m, HEAD_DIM), F32),
            jax.ShapeDtypeStruct((batch, N_HEADS, 1, seq), F32),
        ],
        compiler_params=_params("parallel"),
        name="fox_prep",
    )(tail, bias_row)


def _mla_kernel(q_ref, k_ref, kpe_ref, v_ref, o_ref, m_s, acc_s):
    t = TK_ATT
    n_sub = TQ_ATT // t
    reps = t // HEAD_DIM
    qi = pl.program_id(2)
    ones = jnp.ones((t, HEAD_DIM), BF16)
    m_s[...] = jnp.full(m_s.shape, NEG_INF, F32)
    acc_s[...] = jnp.zeros(acc_s.shape, F32)
    causal = (lax.broadcasted_iota(jnp.int32, (t, t), 1) <= lax.broadcasted_iota(jnp.int32, (t, t), 0))

    def chunk(start, subs):
        k = jnp.concatenate([k_ref[pl.ds(start, t), :], kpe_ref[pl.ds(start, t), :]], axis=1)
        v_aug = jnp.concatenate([v_ref[pl.ds(start, t), :], ones], axis=1)
        rows = [slice(sub * t, (sub + 1) * t) for sub, _ in subs]
        scores = [_dot_nt(q_ref[r, :], k) for r in rows]
        probs, alphas = [], []
        for r, s, (_, diagonal) in zip(rows, scores, subs):
            if diagonal:
                s = jnp.where(causal, s, NEG_INF)
            m_prev = m_s[r, :]
            m_new = jnp.maximum(m_prev, jnp.max(s, axis=-1, keepdims=True))
            m_s[r, :] = m_new
            alphas.append(jnp.exp2(m_prev - m_new))
            probs.append(jnp.exp2(s - _lane_tile(m_new, reps)).astype(BF16))
        for r, p, alpha in zip(rows, probs, alphas):
            acc_s[r, :] = _lane_tile(alpha, 2) * acc_s[r, :] + _dot(p, v_aug)

    def body(c, carry):
        chunk(pl.multiple_of(c * t, t), tuple((sub, False) for sub in range(n_sub)))
        return carry

    lax.fori_loop(0, qi * n_sub, body, 0)
    base = qi * TQ_ATT
    for j in range(n_sub):
        chunk(pl.multiple_of(base + j * t, t), tuple((sub, sub == j) for sub in range(j, n_sub)))
    acc = acc_s[...]
    o_ref[...] = (acc[:, :HEAD_DIM] / acc[:, HEAD_DIM:]).astype(o_ref.dtype)


def _mla_attention(q_a, kv_a, kpe, batch, seq):
    m = q_a.shape[0]
    nq = seq // TQ_ATT
    return pl.pallas_call(
        _mla_kernel,
        grid=(batch, N_HEADS, nq),
        in_specs=[
            pl.BlockSpec((TQ_ATT, MLA_HEAD_PAD), lambda b, h, i: (b * nq + i, h)),
            pl.BlockSpec((seq, HEAD_DIM), lambda b, h, i: (b, h)),
            pl.BlockSpec((seq, HEAD_DIM), lambda b, h, i: (b, 0)),
            pl.BlockSpec((seq, HEAD_DIM), lambda b, h, i: (b, N_HEADS + h)),
        ],
        out_specs=pl.BlockSpec((TQ_ATT, HEAD_DIM), lambda b, h, i: (b * nq + i, h)),
        out_shape=jax.ShapeDtypeStruct((m, GROUP_WIDTH), BF16),
        scratch_shapes=[pltpu.VMEM((TQ_ATT, HEAD_DIM), F32), pltpu.VMEM((TQ_ATT, 2 * HEAD_DIM), F32)],
        compiler_params=_params("parallel", "parallel", "arbitrary"),
        name="mla_attention",
    )(q_a, kv_a, kpe, kv_a)


def _fox_kernel(q_ref, k_ref, v_ref, ccol_ref, crow_ref, o_ref, m_s, acc_s, ct_s, qn_s, *, seq):
    t = TK_ATT
    n_sub = seq // t
    reps = t // HEAD_DIM
    head = pl.program_id(1)
    ones = jnp.ones((t, HEAD_DIM), BF16)
    causal = (lax.broadcasted_iota(jnp.int32, (t, t), 1) <= lax.broadcasted_iota(jnp.int32, (t, t), 0))
    m_s[...] = jnp.full(m_s.shape, NEG_INF, F32)
    acc_s[...] = jnp.zeros(acc_s.shape, F32)
    lane = lax.broadcasted_iota(jnp.int32, (t, HEAD_DIM), 1)
    k_sq_max = jnp.zeros((t, 1), F32)
    for sub in range(n_sub):
        rows = slice(sub * t, (sub + 1) * t)
        c_t = jnp.sum(jnp.where(lane == head, ccol_ref[rows, :], 0.0), axis=1, keepdims=True)
        ct_s[rows, :] = jnp.broadcast_to(c_t * LOG2E, (t, HEAD_DIM))
        q = q_ref[rows, :].astype(F32)
        qn_s[rows, :] = jnp.broadcast_to(jnp.sum(q * q, axis=1, keepdims=True), (t, HEAD_DIM))
        k = k_ref[rows, :].astype(F32)
        k_sq_max = jnp.maximum(k_sq_max, jnp.sum(k * k, axis=1, keepdims=True))
    k_sq = jnp.max(k_sq_max)

    def attend(dist, diagonal):
        static = isinstance(dist, int)
        subs = [sub for sub in range(n_sub) if not static or sub - dist >= 0]
        rows = [slice(sub * t, (sub + 1) * t) for sub in subs]
        chunk_ids = [sub - dist for sub in subs]
        starts = [c * t if static else pl.multiple_of(jnp.maximum(c, 0) * t, t) for c in chunk_ids]
        scores = [_dot_nt(q_ref[r, :], k_ref[pl.ds(st, t), :]) for r, st in zip(rows, starts)]
        probs, alphas = [], []
        for r, s, c, st in zip(rows, scores, chunk_ids, starts):
            s = s - crow_ref[0, 0, :, pl.ds(st, t)] * LOG2E
            if diagonal:
                s = jnp.where(causal, s, NEG_INF)
            if not static:
                s = jnp.where(c >= 0, s, NEG_INF)
            m_prev = m_s[r, :]
            m_new = jnp.maximum(m_prev, jnp.max(s, axis=-1, keepdims=True) + ct_s[r, :])
            m_s[r, :] = m_new
            alphas.append(jnp.exp2(m_prev - m_new))
            probs.append(jnp.exp2(s - _lane_tile(m_new - ct_s[r, :], reps)).astype(BF16))
        for r, p, alpha, st in zip(rows, probs, alphas, starts):
            v_aug = jnp.concatenate([v_ref[pl.ds(st, t), :], ones], axis=1)
            acc_s[r, :] = _lane_tile(alpha, 2) * acc_s[r, :] + _dot(p, v_aug)

    def live_top(dist):
        tops = []
        for sub in range(n_sub):
            rows = slice(sub * t, (sub + 1) * t)
            c = sub - dist
            st = pl.multiple_of(jnp.maximum(c, 0) * t, t)
            c_s_min = jnp.min(crow_ref[0, 0, :, pl.ds(st, t)]) * LOG2E
            slack = jnp.max(jnp.sqrt(qn_s[rows, :] * k_sq) + ct_s[rows, :] - m_s[rows, :])
            tops.append(jnp.where(c >= 0, slack - c_s_min, FOX_DEAD_LOG2))
        return functools.reduce(jnp.maximum, tops)

    attend(0, True)
    for dist in range(1, min(FOX_NEAR_CHUNKS, n_sub)):
        attend(dist, False)

    def more(carry):
        return carry[1] > FOX_DEAD_LOG2

    def body(carry):
        dist = carry[0]
        attend(dist, False)
        return dist + 1, live_top(dist + 1)

    lax.while_loop(more, body, (jnp.int32(FOX_NEAR_CHUNKS), live_top(FOX_NEAR_CHUNKS)))
    acc = acc_s[...]
    o_ref[...] = (acc[:, :HEAD_DIM] / acc[:, HEAD_DIM:]).astype(o_ref.dtype)


def _fox_attention(proj, ccol, crow, batch, seq):
    m = proj.shape[0]
    blk = (seq, HEAD_DIM)
    stat = pltpu.VMEM(blk, F32)
    return pl.pallas_call(
        functools.partial(_fox_kernel, seq=seq),
        grid=(batch, N_HEADS),
        in_specs=[
            pl.BlockSpec(blk, lambda b, h: (b, COL_QC + h)),
            pl.BlockSpec(blk, lambda b, h: (b, COL_KC + h)),
            pl.BlockSpec(blk, lambda b, h: (b, COL_VC + h)),
            pl.BlockSpec(blk, lambda b, h: (b, 0)),
            pl.BlockSpec((1, 1, 1, seq), lambda b, h: (b, h, 0, 0)),
        ],
        out_specs=pl.BlockSpec(blk, lambda b, h: (b, h)),
        out_shape=jax.ShapeDtypeStruct((m, GROUP_WIDTH), BF16),
        scratch_shapes=[stat, pltpu.VMEM((seq, 2 * HEAD_DIM), F32), stat, stat],
        compiler_params=_params("parallel", "parallel"),
        name="fox_attention",
    )(proj, proj, proj, ccol, crow)


def _sb_kernel(q_ref, k_ref, v_ref, o_ref, r_s, acc_s):
    t = T_SB
    n_sub = TQ_SB // t
    reps = t // HEAD_DIM
    first = pl.program_id(2) * n_sub
    row = lax.broadcasted_iota(jnp.int32, (t, t), 0)
    col = lax.broadcasted_iota(jnp.int32, (t, t), 1)
    later = jnp.where(row > col, 1.0, 0.0).astype(BF16)
    past = col < row
    r_s[...] = jnp.zeros(r_s.shape, F32)
    acc_s[...] = jnp.zeros(acc_s.shape, F32)

    def attend(dist, diagonal):
        rows = [slice(sub * t, (sub + 1) * t) for sub in range(n_sub)]
        chunk_ids = [first + (sub - dist) for sub in range(n_sub)]
        always = [isinstance(dist, int) and sub - dist >= 0 for sub in range(n_sub)]
        starts = [pl.multiple_of(jnp.maximum(c, 0) * t, t) for c in chunk_ids]
        logits = [_dot_nt(q_ref[r, :], k_ref[pl.ds(st, t), :]) for r, st in zip(rows, starts)]
        log_betas, log_keeps = [], []
        for z, c, sure in zip(logits, chunk_ids, always):
            neg_abs = pltpu.bitcast(pltpu.bitcast(z, jnp.uint32) | jnp.uint32(0x80000000), F32)
            soft = jnp.log2(1.0 + jnp.exp2(neg_abs))
            log_beta = jnp.minimum(z, 0.0) - soft
            log_keep = log_beta - z
            if diagonal:
                log_keep = jnp.where(past, log_keep, 0.0)
            if not sure:
                log_keep = jnp.where(c >= 0, log_keep, 0.0)
            log_betas.append(log_beta)
            log_keeps.append(log_keep)
        betweens = [_dot(lk.astype(BF16), later) for lk in log_keeps]
        weights = []
        for r, log_beta, between, c, sure in zip(rows, log_betas, betweens, chunk_ids, always):
            a = jnp.exp2(log_beta + (between + _lane_tile(r_s[r, :], reps)))
            if diagonal:
                a = jnp.where(past, a, 0.0)
            if not sure:
                a = jnp.where(c >= 0, a, 0.0)
            weights.append(a.astype(BF16))
        for r, a, log_keep, st in zip(rows, weights, log_keeps, starts):
            acc_s[r, :] += _dot(a, v_ref[pl.ds(st, t), :])
            r_s[r, :] += jnp.sum(log_keep, axis=-1, keepdims=True)

    def live_max(dist):
        tops = [jnp.where(first + sub - dist >= 0, jnp.max(r_s[sub * t:(sub + 1) * t, :]), SB_DEAD_LOG2)
                for sub in range(n_sub)]
        return functools.reduce(jnp.maximum, tops)

    attend(0, True)
    attend(1, False)

    def more(carry):
        return carry[1] > SB_DEAD_LOG2

    def body(carry):
        dist = carry[0]
        attend(dist, False)
        return dist + 1, live_max(dist + 1)

    lax.while_loop(more, body, (jnp.int32(2), live_max(2)))
    o_ref[...] = acc_s[...].astype(o_ref.dtype)


def _sb_attention(proj, batch, seq):
    m = proj.shape[0]
    nq = seq // TQ_SB
    return pl.pallas_call(
        _sb_kernel,
        grid=(batch, N_HEADS, nq),
        in_specs=[
            pl.BlockSpec((TQ_SB, HEAD_DIM), lambda b, h, i: (b * nq + i, COL_QD + h)),
            pl.BlockSpec((seq, HEAD_DIM), lambda b, h, i: (b, COL_KD + h)),
            pl.BlockSpec((seq, HEAD_DIM), lambda b, h, i: (b, COL_VD + h)),
        ],
        out_specs=pl.BlockSpec((TQ_SB, HEAD_DIM), lambda b, h, i: (b * nq + i, h)),
        out_shape=jax.ShapeDtypeStruct((m, GROUP_WIDTH), BF16),
        scratch_shapes=[pltpu.VMEM((TQ_SB, HEAD_DIM), F32), pltpu.VMEM((TQ_SB, HEAD_DIM), F32)],
        compiler_params=_params("parallel", "parallel", "arbitrary"),
        name="sb_attention",
    )(proj, proj, proj)


def _dilated_kernel(q_ref, k_ref, v_ref, cos_ref, sin_ref, o_ref, q_s, k_s, v_s, og_s, lg_s, *, seq):
    for c in range(seq // ROWS_DIL):
        rows = slice(c * ROWS_DIL, (c + 1) * ROWS_DIL)
        cos = cos_ref[rows, :]
        sin = sin_ref[rows, :]
        q_s[rows, :] = _rope128(q_ref[rows, :].astype(F32), cos, sin)
        k_s[rows, :] = _rope128(k_ref[rows, :].astype(F32), cos, sin)
        v_s[rows, :] = v_ref[rows, :].astype(F32)

    ones = jnp.ones((2 * BAND, HEAD_DIM), BF16)
    row1 = lax.broadcasted_iota(jnp.int32, (BAND, BAND), 0)
    col1 = lax.broadcasted_iota(jnp.int32, (BAND, BAND), 1)
    valid1 = col1 <= row1
    row2 = lax.broadcasted_iota(jnp.int32, (BAND, 2 * BAND), 0)
    col2 = lax.broadcasted_iota(jnp.int32, (BAND, 2 * BAND), 1)
    dist = BAND + row2 - col2
    valid2 = (dist >= 0) & (dist <= BAND)

    def attend(g, d, q_start, k_start, n_keys, valid):
        def strided(start, n):
            return pl.ds(start, n, stride=d) if d > 1 else pl.ds(start, n)
        qb = q_s[strided(q_start, BAND), :].astype(BF16)
        kb = k_s[strided(k_start, n_keys), :].astype(BF16)
        vb = v_s[strided(k_start, n_keys), :].astype(BF16)
        s = jnp.where(valid, _dot_nt(qb, kb), NEG_INF)
        m = jnp.max(s, axis=-1, keepdims=True)
        p = jnp.exp2(s - m)
        pv = _dot(p.astype(BF16), jnp.concatenate([vb, ones[:n_keys]], axis=1))
        den = pv[:, HEAD_DIM:]
        og_s[g, strided(q_start, BAND), :] = pv[:, :HEAD_DIM] / den
        lg_s[g, strided(q_start, BAND), :] = m + jnp.log2(den)

    for g, d in enumerate(DILATIONS):
        nb = seq // d // BAND

        def residue(r, carry, g=g, d=d, nb=nb):
            attend(g, d, r, r, BAND, valid1)

            def block(i, c2):
                attend(g, d, r + i * (BAND * d), r + (i - 1) * (BAND * d), 2 * BAND, valid2)
                return c2

            lax.fori_loop(1, nb, block, 0, unroll=DIL_UNROLL if nb > 2 else 1)
            return carry

        lax.fori_loop(0, d, residue, 0, unroll=max(1, min(d, 2 * DIL_UNROLL // nb)))

    for c in range(seq // ROWS_DIL):
        rows = slice(c * ROWS_DIL, (c + 1) * ROWS_DIL)
        l1, l2, l3 = lg_s[0, rows, :], lg_s[1, rows, :], lg_s[2, rows, :]
        top = jnp.maximum(jnp.maximum(l1, l2), l3)
        e1, e2, e3 = jnp.exp2(l1 - top), jnp.exp2(l2 - top), jnp.exp2(l3 - top)
        num = (e1 * og_s[0, rows, :] + e2 * og_s[1, rows, :]) + e3 * og_s[2, rows, :]
        o_ref[rows, :] = (num / ((e1 + e2) + e3)).astype(o_ref.dtype)


def _dilated_attention(proj, cos_b, sin_b, batch, seq):
    m = proj.shape[0]
    blk = (seq, HEAD_DIM)
    n_groups = len(DILATIONS)
    return pl.pallas_call(
        functools.partial(_dilated_kernel, seq=seq),
        grid=(batch, N_HEADS),
        in_specs=[
            pl.BlockSpec(blk, lambda b, h: (b, COL_QB + h)),
            pl.BlockSpec(blk, lambda b, h: (b, COL_KB + h)),
            pl.BlockSpec(blk, lambda b, h: (b, COL_VB + h)),
            pl.BlockSpec(blk, lambda b, h: (0, 0)),
            pl.BlockSpec(blk, lambda b, h: (0, 0)),
        ],
        out_specs=pl.BlockSpec(blk, lambda b, h: (b, h)),
        out_shape=jax.ShapeDtypeStruct((m, GROUP_WIDTH), BF16),
        scratch_shapes=[pltpu.VMEM(blk, F32), pltpu.VMEM(blk, F32), pltpu.VMEM(blk, F32),
                        pltpu.VMEM((n_groups,) + blk, F32), pltpu.VMEM((n_groups,) + blk, F32)],
        compiler_params=_params("parallel", "parallel"),
        name="dilated_attention",
    )(proj, proj, proj, cos_b, sin_b)


def _out_proj_kernel(oa_ref, ob_ref, oc_ref, od_ref, gn_ref, w_ref, x_ref, o_ref):
    for sub in range(TM_OUT // ROWS_OUT):
        rows = slice(sub * ROWS_OUT, (sub + 1) * ROWS_OUT)
        parts = []
        for group, ref in enumerate((oa_ref, ob_ref, oc_ref, od_ref)):
            cols = slice(group * GROUP_WIDTH, (group + 1) * GROUP_WIDTH)
            parts.append(_rms(ref[rows, :].astype(F32), gn_ref[:, cols]).astype(BF16))
        g = jnp.concatenate(parts, axis=1)
        o_ref[rows, :] = x_ref[rows, :] + _dot(g, w_ref[...])


def _out_proj(oa, ob, oc, od, gn, w_out, layer, x):
    m = x.shape[0]
    grp = pl.BlockSpec((TM_OUT, GROUP_WIDTH), lambda i: (i, 0))
    return pl.pallas_call(
        _out_proj_kernel,
        grid=(m // TM_OUT,),
        in_specs=[
            grp, grp, grp, grp,
            pl.BlockSpec((1, D_MODEL), lambda i: (0, 0)),
            pl.BlockSpec((None, D_MODEL, D_MODEL), lambda i: (layer, 0, 0)),
            pl.BlockSpec((TM_OUT, D_MODEL), lambda i: (i, 0)),
        ],
        out_specs=pl.BlockSpec((TM_OUT, D_MODEL), lambda i: (i, 0)),
        out_shape=jax.ShapeDtypeStruct((m, D_MODEL), F32),
        compiler_params=_params("parallel"),
        name="out_proj",
    )(oa, ob, oc, od, gn, w_out, x)


def _ffn_kernel(x_ref, g_ref, wg_ref, wu_ref, wd_ref, fg_ref, o_ref, h_s, *, final):
    f = pl.program_id(1)

    @pl.when(f == 0)
    def _():
        x = x_ref[...]
        h_s[...] = _rms(x, g_ref[...]).astype(BF16)
        o_ref[...] = x

    h = h_s[...]
    gate = _dot(h, wg_ref[...])
    up = _dot(h, wu_ref[...])
    act = (gate / (1.0 + jnp.exp(-gate))) * up
    o_ref[...] += _dot(act.astype(BF16), wd_ref[...])

    if final:
        @pl.when(f == pl.num_programs(1) - 1)
        def _():
            o_ref[...] = _rms(o_ref[...], fg_ref[...])


def _ffn(x, gain, w_gate, w_up, w_down, layer, final_gain, final):
    m = x.shape[0]
    return pl.pallas_call(
        functools.partial(_ffn_kernel, final=final),
        grid=(m // TM_FFN, FFN_HIDDEN // TF_FFN),
        in_specs=[
            pl.BlockSpec((TM_FFN, D_MODEL), lambda i, f: (i, 0)),
            pl.BlockSpec((1, D_MODEL), lambda i, f: (0, 0)),
            pl.BlockSpec((None, D_MODEL, TF_FFN), lambda i, f: (layer, 0, f)),
            pl.BlockSpec((None, D_MODEL, TF_FFN), lambda i, f: (layer, 0, f)),
            pl.BlockSpec((None, TF_FFN, D_MODEL), lambda i, f: (layer, f, 0)),
            pl.BlockSpec((1, D_MODEL), lambda i, f: (0, 0)),
        ],
        out_specs=pl.BlockSpec((TM_FFN, D_MODEL), lambda i, f: (i, 0)),
        out_shape=jax.ShapeDtypeStruct((m, D_MODEL), F32),
        scratch_shapes=[pltpu.VMEM((TM_FFN, D_MODEL), BF16)],
        compiler_params=_params("parallel", "arbitrary"),
        name="ffn_final" if final else "ffn",
    )(x, gain, w_gate, w_up, w_down, final_gain)


def _rot_cols(w):
    half = w.shape[-1] // 2
    return jnp.concatenate([-w[..., half:], w[..., :half]], axis=-1)


def _w_in_kernel(w_ref, s_ref, main_ref, tail_ref):
    w = w_ref[...] * s_ref[...]
    o = IN_OFFSETS
    main_ref[...] = jnp.concatenate([w[:, :o[2]], w[:, o[3]:o[9]], w[:, o[10]:]], axis=1).astype(BF16)
    k_rope = w[:, o[2]:o[3]]
    pad = jnp.zeros((w.shape[0], HEAD_DIM - N_HEADS), F32)
    tail_ref[...] = jnp.concatenate([k_rope, _rot_cols(k_rope), w[:, o[9]:o[10]], pad], axis=1).astype(BF16)


def _prep_w_in(w_in, col_scale):
    depth, _, width = w_in.shape
    return pl.pallas_call(
        _w_in_kernel,
        grid=(depth, D_MODEL // TK_PREP),
        in_specs=[
            pl.BlockSpec((None, TK_PREP, width), lambda l, i: (l, i, 0)),
            pl.BlockSpec((1, width), lambda l, i: (0, 0)),
        ],
        out_specs=[
            pl.BlockSpec((None, TK_PREP, MAIN_WIDTH), lambda l, i: (l, i, 0)),
            pl.BlockSpec((None, TK_PREP, TAIL_WIDTH), lambda l, i: (l, i, 0)),
        ],
        out_shape=[
            jax.ShapeDtypeStruct((depth, D_MODEL, MAIN_WIDTH), BF16),
            jax.ShapeDtypeStruct((depth, D_MODEL, TAIL_WIDTH), BF16),
        ],
        compiler_params=_params("parallel", "parallel"),
        name="prep_w_in",
    )(w_in, col_scale)


def _prep_weights(w_in, w_uq, w_ukv, fox_forget_bias):
    depth = w_in.shape[0]
    col_scale = np.ones((1, IN_OFFSETS[-1]), np.float32)
    for q_split in (3, 6, 10):
        col_scale[:, IN_OFFSETS[q_split]:IN_OFFSETS[q_split + 1]] = HEAD_DIM ** -0.5 * LOG2E
    w_main, w_tail = _prep_w_in(w_in, jnp.asarray(col_scale))
    uq = (w_uq * (MLA_QK_DIM ** -0.5 * LOG2E)).reshape(depth, Q_LORA, N_HEADS, MLA_QK_DIM)
    pe = uq[..., QK_NOPE:]
    uq = jnp.concatenate([uq[..., :QK_NOPE], pe, _rot_cols(pe)], axis=-1)
    uq = uq.reshape(depth, Q_LORA, N_HEADS * MLA_HEAD_PAD)
    ukv = w_ukv.reshape(depth, KV_LORA, N_HEADS, 2, HEAD_DIM).transpose(0, 1, 3, 2, 4)
    ukv = ukv.reshape(depth, KV_LORA, 2 * GROUP_WIDTH)
    bias_rows = jnp.pad(fox_forget_bias, ((0, 0), (0, HEAD_DIM - N_HEADS)))[:, None, :]
    return w_main, w_tail, uq.astype(BF16), ukv.astype(BF16), bias_rows


def _rope_tables(seq):
    pos = jnp.arange(seq, dtype=F32)[:, None]

    def angles(dim):
        inv_freq = ROPE_THETA ** (-jnp.arange(0, dim, 2, dtype=F32) / dim)
        return pos * inv_freq[None, :]

    ang_b = angles(HEAD_DIM)
    cos_b = jnp.concatenate([jnp.cos(ang_b)] * 2, axis=1)
    sin_b = jnp.concatenate([-jnp.sin(ang_b), jnp.sin(ang_b)], axis=1)
    ang_a = angles(QK_ROPE)
    zeros = jnp.zeros((seq, HEAD_DIM - QK_ROPE), F32)
    cos_a = jnp.concatenate([jnp.cos(ang_a)] * 2 + [zeros], axis=1)
    sin_a = jnp.concatenate([jnp.sin(ang_a)] * 2 + [zeros], axis=1)
    return cos_a, sin_a, cos_b, sin_b


def kernel(x, attn_norm, w_in, mla_q_norm, w_uq, mla_kv_norm, w_ukv, fox_forget_bias, group_norm,
           w_out, ffn_norm, w_gate, w_up, w_down, final_norm):
    batch, seq, _ = x.shape
    m = batch * seq
    assert seq % TQ_ATT == 0 and seq % (max(DILATIONS) * BAND) == 0 and m % TM_IN == 0
    w_main, w_tail, uq, ukv, bias_rows = _prep_weights(w_in, w_uq, w_ukv, fox_forget_bias)
    w_out_b, w_gate_b, w_up_b, w_down_b = (w.astype(BF16) for w in (w_out, w_gate, w_up, w_down))
    cos_a, sin_a, cos_b, sin_b = _rope_tables(seq)
    row = lambda v: v.reshape(1, -1)

    xf = x.reshape(m, D_MODEL)
    for l in range(DEPTH):
        proj, tail = _in_proj(xf, row(attn_norm[l]), w_main, w_tail, l)
        q_a, kv_a, kpe = _mla_prep(proj, tail, row(mla_q_norm[l]), row(mla_kv_norm[l]), uq, ukv, l,
                                   cos_a, sin_a, seq)
        ccol, crow = _fox_prep(tail, bias_rows[l], batch, seq)
        out_a = _mla_attention(q_a, kv_a, kpe, batch, seq)
        out_b = _dilated_attention(proj, cos_b, sin_b, batch, seq)
        out_c = _fox_attention(proj, ccol, crow, batch, seq)
        out_d = _sb_attention(proj, batch, seq)
        xf = _out_proj(out_a, out_b, out_c, out_d, row(group_norm[l]), w_out_b, l, xf)
        xf = _ffn(xf, row(ffn_norm[l]), w_gate_b, w_up_b, w_down_b, l, row(final_norm),
                  final=(l == DEPTH - 1))
    return xf.reshape(batch, seq, D_MODEL)
```

```python
import functools
import math

import jax
import jax.numpy as jnp
import numpy as np
from jax import lax
from jax.experimental import pallas as pl
from jax.experimental.pallas import tpu as pltpu

F32 = jnp.float32
BF16 = jnp.bfloat16

D_MODEL = 2048
DEPTH = 4
N_HEADS = 4
HEAD_DIM = 128
GROUP_WIDTH = N_HEADS * HEAD_DIM
Q_LORA = 512
KV_LORA = 512
QK_NOPE = 128
QK_ROPE = 64
MLA_QK_DIM = QK_NOPE + QK_ROPE
MLA_HEAD_PAD = 2 * HEAD_DIM
DILATED_PAIRS = ((128, 1), (512, 4), (2048, 16))
BAND = 128
assert all(window // dilation == BAND for window, dilation in DILATED_PAIRS)
DILATIONS = tuple(dilation for _, dilation in DILATED_PAIRS)
ROPE_THETA = 10000.0
FFN_HIDDEN = 5632
EPS = 1e-6
NEG_INF = -1e30
LOG2E = math.log2(math.e)

IN_SPLITS = ((Q_LORA, KV_LORA, QK_ROPE) + (GROUP_WIDTH,) * 3 + (GROUP_WIDTH,) * 3 + (N_HEADS,)
             + (GROUP_WIDTH,) * 3)
IN_OFFSETS = tuple(int(v) for v in np.cumsum((0,) + IN_SPLITS))

MAIN_WIDTH = Q_LORA + KV_LORA + 9 * GROUP_WIDTH
TAIL_WIDTH = 2 * HEAD_DIM
COL_QB, COL_KB, COL_VB = 8, 12, 16
COL_QC, COL_KC, COL_VC = 20, 24, 28
COL_QD, COL_KD, COL_VD = 32, 36, 40

VMEM_LIMIT_BYTES = 56 * 1024 * 1024
TM_IN, TN_IN = 512, 512
TM_MLA = 1024
TM_OUT = 512
ROWS_OUT = 128
TM_FFN, TF_FFN = 1024, 512
TQ_ATT, TK_ATT = 4096, 512
TQ_SB, T_SB = 1024, 256
SB_DEAD_LOG2 = -150.0
FOX_DEAD_LOG2 = -150.0
FOX_NEAR_CHUNKS = 3
TK_PREP = 256
ROWS_DIL = 512
DIL_UNROLL = 8
CUM_CHUNK = 256


def _params(*sem):
    return pltpu.CompilerParams(dimension_semantics=sem, vmem_limit_bytes=VMEM_LIMIT_BYTES)


def _rms(x, gain):
    return (x * lax.rsqrt(jnp.mean(x * x, axis=-1, keepdims=True) + EPS)) * gain


def _dot(a, b):
    return jnp.dot(a, b, preferred_element_type=F32)


def _dot_nt(a, b):
    return lax.dot_general(a, b, (((1,), (1,)), ((), ())), preferred_element_type=F32)


def _lane_tile(x, n):
    return jnp.concatenate([x] * n, axis=1)


def _rope128(y, cos, sin):
    return y * cos + pltpu.roll(y, 64, 1) * sin


def _in_proj_kernel(x_ref, g_ref, w_ref, wt_ref, o_ref, ot_ref):
    hb = _rms(x_ref[...], g_ref[...]).astype(BF16)
    ot_ref[...] = _dot(hb, wt_ref[...])
    for j in range(MAIN_WIDTH // TN_IN):
        cols = slice(j * TN_IN, (j + 1) * TN_IN)
        o_ref[:, cols] = _dot(hb, w_ref[:, cols]).astype(o_ref.dtype)


def _in_proj(x, gain, w_main, w_tail, layer):
    m = x.shape[0]
    resident = dict(pipeline_mode=pl.Buffered(1))
    return pl.pallas_call(
        _in_proj_kernel,
        grid=(m // TM_IN,),
        in_specs=[
            pl.BlockSpec((TM_IN, D_MODEL), lambda i: (i, 0)),
            pl.BlockSpec((1, D_MODEL), lambda i: (0, 0)),
            pl.BlockSpec((None, D_MODEL, MAIN_WIDTH), lambda i: (layer, 0, 0), **resident),
            pl.BlockSpec((None, D_MODEL, TAIL_WIDTH), lambda i: (layer, 0, 0), **resident),
        ],
        out_specs=[
            pl.BlockSpec((TM_IN, MAIN_WIDTH), lambda i: (i, 0)),
            pl.BlockSpec((TM_IN, TAIL_WIDTH), lambda i: (i, 0)),
        ],
        out_shape=[
            jax.ShapeDtypeStruct((m, MAIN_WIDTH), BF16),
            jax.ShapeDtypeStruct((m, TAIL_WIDTH), F32),
        ],
        compiler_params=_params("parallel"),
        name="in_proj",
    )(x, gain, w_main, w_tail)


def _mla_prep_kernel(ql_ref, kvl_ref, t_ref, gq_ref, gkv_ref, wq_ref, wkv_ref, cos_ref, sin_ref,
                     q_ref, kv_ref, kpe_ref):
    cos = cos_ref[...]
    sin = sin_ref[...]
    hq = _rms(ql_ref[...].astype(F32), gq_ref[...]).astype(BF16)
    q = _dot(hq, wq_ref[...])
    for h in range(N_HEADS):
        lo = h * MLA_HEAD_PAD
        q_ref[:, lo:lo + HEAD_DIM] = q[:, lo:lo + HEAD_DIM].astype(BF16)
        y = q[:, lo + HEAD_DIM:lo + MLA_HEAD_PAD]
        q_ref[:, lo + HEAD_DIM:lo + MLA_HEAD_PAD] = _rope128(y, cos, sin).astype(BF16)
    hkv = _rms(kvl_ref[...].astype(F32), gkv_ref[...]).astype(BF16)
    kv_ref[...] = _dot(hkv, wkv_ref[...]).astype(BF16)
    kpe_ref[...] = _rope128(t_ref[...], cos, sin).astype(BF16)


def _mla_prep(proj, tail, gq, gkv, wq, wkv, layer, cos_t, sin_t, seq):
    m = proj.shape[0]
    nseq = seq // TM_MLA
    return pl.pallas_call(
        _mla_prep_kernel,
        grid=(m // TM_MLA,),
        in_specs=[
            pl.BlockSpec((TM_MLA, Q_LORA), lambda i: (i, 0)),
            pl.BlockSpec((TM_MLA, KV_LORA), lambda i: (i, 1)),
            pl.BlockSpec((TM_MLA, HEAD_DIM), lambda i: (i, 0)),
            pl.BlockSpec((1, Q_LORA), lambda i: (0, 0)),
            pl.BlockSpec((1, KV_LORA), lambda i: (0, 0)),
            pl.BlockSpec((None, Q_LORA, N_HEADS * MLA_HEAD_PAD), lambda i: (layer, 0, 0)),
            pl.BlockSpec((None, KV_LORA, 2 * GROUP_WIDTH), lambda i: (layer, 0, 0)),
            pl.BlockSpec((TM_MLA, HEAD_DIM), lambda i: (i % nseq, 0)),
            pl.BlockSpec((TM_MLA, HEAD_DIM), lambda i: (i % nseq, 0)),
        ],
        out_specs=[
            pl.BlockSpec((TM_MLA, N_HEADS * MLA_HEAD_PAD), lambda i: (i, 0)),
            pl.BlockSpec((TM_MLA, 2 * GROUP_WIDTH), lambda i: (i, 0)),
            pl.BlockSpec((TM_MLA, HEAD_DIM), lambda i: (i, 0)),
        ],
        out_shape=[
            jax.ShapeDtypeStruct((m, N_HEADS * MLA_HEAD_PAD), BF16),
            jax.ShapeDtypeStruct((m, 2 * GROUP_WIDTH), BF16),
            jax.ShapeDtypeStruct((m, HEAD_DIM), BF16),
        ],
        compiler_params=_params("parallel"),
        name="mla_prep",
    )(proj, proj, tail, gq, gkv, wq, wkv, cos_t, sin_t)


def _split3(x):
    hi = x.astype(BF16)
    r1 = x - hi.astype(F32)
    mid = r1.astype(BF16)
    lo = (r1 - mid.astype(F32)).astype(BF16)
    return hi, mid, lo


def _fox_prep_kernel(t_ref, b_ref, col_ref, row_ref, *, seq):
    r = lax.broadcasted_iota(jnp.int32, (CUM_CHUNK, CUM_CHUNK), 0)
    c = lax.broadcasted_iota(jnp.int32, (CUM_CHUNK, CUM_CHUNK), 1)
    lower = jnp.where(c <= r, 1.0, 0.0).astype(BF16)
    carry = jnp.zeros((1, HEAD_DIM), F32)
    for ci in range(seq // CUM_CHUNK):
        rows = slice(ci * CUM_CHUNK, (ci + 1) * CUM_CHUNK)
        x = t_ref[rows, :] + b_ref[...]
        log_f = jnp.minimum(x, 0.0) - jnp.log(1.0 + jnp.exp(-jnp.abs(x)))
        hi, mid, lo = _split3(log_f)
        cum = (_dot(lower, hi) + _dot(lower, mid)) + _dot(lower, lo) + carry
        col_ref[rows, :] = cum
        cum_t = cum.T
        for h in range(N_HEADS):
            row_ref[0, h, :, rows] = cum_t[h:h + 1, :]
        carry = cum[CUM_CHUNK - 1:CUM_CHUNK, :]


def _fox_prep(tail, bias_row, batch, seq):
    m = tail.shape[0]
    return pl.pallas_call(
        functools.partial(_fox_prep_kernel, seq=seq),
        grid=(batch,),
        in_specs=[
            pl.BlockSpec((seq, HEAD_DIM), lambda b: (b, 1)),
            pl.BlockSpec((1, HEAD_DIM), lambda b: (0, 0)),
        ],
        out_specs=[
            pl.BlockSpec((seq, HEAD_DIM), lambda b: (b, 0)),
            pl.BlockSpec((1, N_HEADS, 1, seq), lambda b: (b, 0, 0, 0)),
        ],
        out_shape=[
            jax.ShapeDtypeStruct((m, HEAD_DIM), F32),
            jax.ShapeDtypeStruct((batch, N_HEADS, 1, seq), F32),
        ],
        compiler_params=_params("parallel"),
        name="fox_prep",
    )(tail, bias_row)


def _diagonal_masks(t):
    half = t // 2
    top = lax.broadcasted_iota(jnp.int32, (half, half), 1) <= lax.broadcasted_iota(jnp.int32, (half, half), 0)
    bottom = (lax.broadcasted_iota(jnp.int32, (half, t), 1)
              <= lax.broadcasted_iota(jnp.int32, (half, t), 0) + half)
    return top, bottom


def _mla_kernel(q_ref, k_ref, kpe_ref, v_ref, o_ref, m_s, acc_s):
    t = TK_ATT
    n_sub = TQ_ATT // t
    qi = pl.program_id(2)
    ones = jnp.ones((t, HEAD_DIM), BF16)
    m_s[...] = jnp.full(m_s.shape, NEG_INF, F32)
    acc_s[...] = jnp.zeros(acc_s.shape, F32)
    half = t // 2
    masks = _diagonal_masks(t)

    def chunk(start, subs):
        k = jnp.concatenate([k_ref[pl.ds(start, t), :], kpe_ref[pl.ds(start, t), :]], axis=1)
        v_aug = jnp.concatenate([v_ref[pl.ds(start, t), :], ones], axis=1)
        units = []
        for sub, diagonal in subs:
            if diagonal:
                units.append((slice(sub * t, sub * t + half), half, masks[0]))
                units.append((slice(sub * t + half, (sub + 1) * t), t, masks[1]))
            else:
                units.append((slice(sub * t, (sub + 1) * t), t, None))
        scores = [_dot_nt(q_ref[r, :], k[:n_keys, :]) for r, n_keys, _ in units]
        probs, alphas = [], []
        for (r, n_keys, mask), s in zip(units, scores):
            if mask is not None:
                s = jnp.where(mask, s, NEG_INF)
            m_prev = m_s[r, :]
            m_new = jnp.maximum(m_prev, jnp.max(s, axis=-1, keepdims=True))
            m_s[r, :] = m_new
            alphas.append(jnp.exp2(m_prev - m_new))
            probs.append(jnp.exp2(s - _lane_tile(m_new, n_keys // HEAD_DIM)).astype(BF16))
        for (r, n_keys, _), p, alpha in zip(units, probs, alphas):
            acc_s[r, :] = _lane_tile(alpha, 2) * acc_s[r, :] + _dot(p, v_aug[:n_keys, :])

    def body(c, carry):
        chunk(pl.multiple_of(c * t, t), tuple((sub, False) for sub in range(n_sub)))
        return carry

    lax.fori_loop(0, qi * n_sub, body, 0)
    base = qi * TQ_ATT
    for j in range(n_sub):
        chunk(pl.multiple_of(base + j * t, t), tuple((sub, sub == j) for sub in range(j, n_sub)))
    acc = acc_s[...]
    o_ref[...] = (acc[:, :HEAD_DIM] / acc[:, HEAD_DIM:]).astype(o_ref.dtype)


def _mla_attention(q_a, kv_a, kpe, batch, seq):
    m = q_a.shape[0]
    nq = seq // TQ_ATT
    return pl.pallas_call(
        _mla_kernel,
        grid=(batch, N_HEADS, nq),
        in_specs=[
            pl.BlockSpec((TQ_ATT, MLA_HEAD_PAD), lambda b, h, i: (b * nq + i, h)),
            pl.BlockSpec((seq, HEAD_DIM), lambda b, h, i: (b, h)),
            pl.BlockSpec((seq, HEAD_DIM), lambda b, h, i: (b, 0)),
            pl.BlockSpec((seq, HEAD_DIM), lambda b, h, i: (b, N_HEADS + h)),
        ],
        out_specs=pl.BlockSpec((TQ_ATT, HEAD_DIM), lambda b, h, i: (b * nq + i, h)),
        out_shape=jax.ShapeDtypeStruct((m, GROUP_WIDTH), BF16),
        scratch_shapes=[pltpu.VMEM((TQ_ATT, HEAD_DIM), F32), pltpu.VMEM((TQ_ATT, 2 * HEAD_DIM), F32)],
        compiler_params=_params("parallel", "parallel", "arbitrary"),
        name="mla_attention",
    )(q_a, kv_a, kpe, kv_a)


def _fox_kernel(q_ref, k_ref, v_ref, ccol_ref, crow_ref, o_ref, m_s, acc_s, ct_s, qn_s, *, seq):
    t = TK_ATT
    n_sub = seq // t
    half = t // 2
    masks = _diagonal_masks(t)
    head = pl.program_id(1)
    ones = jnp.ones((t, HEAD_DIM), BF16)
    m_s[...] = jnp.full(m_s.shape, NEG_INF, F32)
    acc_s[...] = jnp.zeros(acc_s.shape, F32)
    lane = lax.broadcasted_iota(jnp.int32, (t, HEAD_DIM), 1)
    k_sq_max = jnp.zeros((t, 1), F32)
    for sub in range(n_sub):
        rows = slice(sub * t, (sub + 1) * t)
        c_t = jnp.sum(jnp.where(lane == head, ccol_ref[rows, :], 0.0), axis=1, keepdims=True)
        ct_s[rows, :] = jnp.broadcast_to(c_t * LOG2E, (t, HEAD_DIM))
        q = q_ref[rows, :].astype(F32)
        qn_s[rows, :] = jnp.broadcast_to(jnp.sum(q * q, axis=1, keepdims=True), (t, HEAD_DIM))
        k = k_ref[rows, :].astype(F32)
        k_sq_max = jnp.maximum(k_sq_max, jnp.sum(k * k, axis=1, keepdims=True))
    k_sq = jnp.max(k_sq_max)

    def attend(dist, diagonal):
        static = isinstance(dist, int)
        units = []
        for sub in range(n_sub):
            c = sub - dist
            if static and c < 0:
                continue
            st = c * t if static else pl.multiple_of(jnp.maximum(c, 0) * t, t)
            if diagonal:
                units.append((slice(sub * t, sub * t + half), c, st, half, masks[0]))
                units.append((slice(sub * t + half, (sub + 1) * t), c, st, t, masks[1]))
            else:
                units.append((slice(sub * t, (sub + 1) * t), c, st, t, None))
        scores = [_dot_nt(q_ref[r, :], k_ref[pl.ds(st, n), :]) for r, _, st, n, _ in units]
        probs, alphas = [], []
        for (r, c, st, n, mask), s in zip(units, scores):
            s = s - crow_ref[0, 0, :, pl.ds(st, n)] * LOG2E
            if mask is not None:
                s = jnp.where(mask, s, NEG_INF)
            if not static:
                s = jnp.where(c >= 0, s, NEG_INF)
            m_prev = m_s[r, :]
            m_new = jnp.maximum(m_prev, jnp.max(s, axis=-1, keepdims=True) + ct_s[r, :])
            m_s[r, :] = m_new
            alphas.append(jnp.exp2(m_prev - m_new))
            probs.append(jnp.exp2(s - _lane_tile(m_new - ct_s[r, :], n // HEAD_DIM)).astype(BF16))
        for (r, _, st, n, _), p, alpha in zip(units, probs, alphas):
            v_aug = jnp.concatenate([v_ref[pl.ds(st, n), :], ones[:n]], axis=1)
            acc_s[r, :] = _lane_tile(alpha, 2) * acc_s[r, :] + _dot(p, v_aug)

    def live_top(dist):
        tops = []
        for sub in range(n_sub):
            rows = slice(sub * t, (sub + 1) * t)
            c = sub - dist
            st = pl.multiple_of(jnp.maximum(c, 0) * t, t)
            c_s_min = jnp.min(crow_ref[0, 0, :, pl.ds(st, t)]) * LOG2E
            slack = jnp.max(jnp.sqrt(qn_s[rows, :] * k_sq) + ct_s[rows, :] - m_s[rows, :])
            tops.append(jnp.where(c >= 0, slack - c_s_min, FOX_DEAD_LOG2))
        return functools.reduce(jnp.maximum, tops)

    attend(0, True)
    for dist in range(1, min(FOX_NEAR_CHUNKS, n_sub)):
        attend(dist, False)

    def more(carry):
        return carry[1] > FOX_DEAD_LOG2

    def body(carry):
        dist = carry[0]
        attend(dist, False)
        return dist + 1, live_top(dist + 1)

    lax.while_loop(more, body, (jnp.int32(FOX_NEAR_CHUNKS), live_top(FOX_NEAR_CHUNKS)))
    acc = acc_s[...]
    o_ref[...] = (acc[:, :HEAD_DIM] / acc[:, HEAD_DIM:]).astype(o_ref.dtype)


def _fox_attention(proj, ccol, crow, batch, seq):
    m = proj.shape[0]
    blk = (seq, HEAD_DIM)
    stat = pltpu.VMEM(blk, F32)
    return pl.pallas_call(
        functools.partial(_fox_kernel, seq=seq),
        grid=(batch, N_HEADS),
        in_specs=[
            pl.BlockSpec(blk, lambda b, h: (b, COL_QC + h)),
            pl.BlockSpec(blk, lambda b, h: (b, COL_KC + h)),
            pl.BlockSpec(blk, lambda b, h: (b, COL_VC + h)),
            pl.BlockSpec(blk, lambda b, h: (b, 0)),
            pl.BlockSpec((1, 1, 1, seq), lambda b, h: (b, h, 0, 0)),
        ],
        out_specs=pl.BlockSpec(blk, lambda b, h: (b, h)),
        out_shape=jax.ShapeDtypeStruct((m, GROUP_WIDTH), BF16),
        scratch_shapes=[stat, pltpu.VMEM((seq, 2 * HEAD_DIM), F32), stat, stat],
        compiler_params=_params("parallel", "parallel"),
        name="fox_attention",
    )(proj, proj, proj, ccol, crow)


def _sb_kernel(q_ref, k_ref, v_ref, o_ref, r_s, acc_s):
    t = T_SB
    n_sub = TQ_SB // t
    reps = t // HEAD_DIM
    first = pl.program_id(2) * n_sub
    row = lax.broadcasted_iota(jnp.int32, (t, t), 0)
    col = lax.broadcasted_iota(jnp.int32, (t, t), 1)
    later = jnp.where(row > col, 1.0, 0.0).astype(BF16)
    past = col < row
    r_s[...] = jnp.zeros(r_s.shape, F32)
    acc_s[...] = jnp.zeros(acc_s.shape, F32)

    def attend(dist, diagonal):
        rows = [slice(sub * t, (sub + 1) * t) for sub in range(n_sub)]
        chunk_ids = [first + (sub - dist) for sub in range(n_sub)]
        always = [isinstance(dist, int) and sub - dist >= 0 for sub in range(n_sub)]
        starts = [pl.multiple_of(jnp.maximum(c, 0) * t, t) for c in chunk_ids]
        logits = [_dot_nt(q_ref[r, :], k_ref[pl.ds(st, t), :]) for r, st in zip(rows, starts)]
        log_betas, log_keeps = [], []
        for z, c, sure in zip(logits, chunk_ids, always):
            neg_abs = pltpu.bitcast(pltpu.bitcast(z, jnp.uint32) | jnp.uint32(0x80000000), F32)
            soft = jnp.log2(1.0 + jnp.exp2(neg_abs))
            log_beta = jnp.minimum(z, 0.0) - soft
            log_keep = log_beta - z
            if diagonal:
                log_keep = jnp.where(past, log_keep, 0.0)
            if not sure:
                log_keep = jnp.where(c >= 0, log_keep, 0.0)
            log_betas.append(log_beta)
            log_keeps.append(log_keep)
        betweens = [_dot(lk.astype(BF16), later) for lk in log_keeps]
        weights = []
        for r, log_beta, between, c, sure in zip(rows, log_betas, betweens, chunk_ids, always):
            a = jnp.exp2(log_beta + (between + _lane_tile(r_s[r, :], reps)))
            if diagonal:
                a = jnp.where(past, a, 0.0)
            if not sure:
                a = jnp.where(c >= 0, a, 0.0)
            weights.append(a.astype(BF16))
        for r, a, log_keep, st in zip(rows, weights, log_keeps, starts):
            acc_s[r, :] += _dot(a, v_ref[pl.ds(st, t), :])
            r_s[r, :] += jnp.sum(log_keep, axis=-1, keepdims=True)

    def live_max(dist):
        tops = [jnp.where(first + sub - dist >= 0, jnp.max(r_s[sub * t:(sub + 1) * t, :]), SB_DEAD_LOG2)
                for sub in range(n_sub)]
        return functools.reduce(jnp.maximum, tops)

    attend(0, True)
    attend(1, False)

    def more(carry):
        return carry[1] > SB_DEAD_LOG2

    def body(carry):
        dist = carry[0]
        attend(dist, False)
        return dist + 1, live_max(dist + 1)

    lax.while_loop(more, body, (jnp.int32(2), live_max(2)))
    o_ref[...] = acc_s[...].astype(o_ref.dtype)


def _sb_attention(proj, batch, seq):
    m = proj.shape[0]
    nq = seq // TQ_SB
    return pl.pallas_call(
        _sb_kernel,
        grid=(batch, N_HEADS, nq),
        in_specs=[
            pl.BlockSpec((TQ_SB, HEAD_DIM), lambda b, h, i: (b * nq + i, COL_QD + h)),
            pl.BlockSpec((seq, HEAD_DIM), lambda b, h, i: (b, COL_KD + h)),
            pl.BlockSpec((seq, HEAD_DIM), lambda b, h, i: (b, COL_VD + h)),
        ],
        out_specs=pl.BlockSpec((TQ_SB, HEAD_DIM), lambda b, h, i: (b * nq + i, h)),
        out_shape=jax.ShapeDtypeStruct((m, GROUP_WIDTH), BF16),
        scratch_shapes=[pltpu.VMEM((TQ_SB, HEAD_DIM), F32), pltpu.VMEM((TQ_SB, HEAD_DIM), F32)],
        compiler_params=_params("parallel", "parallel", "arbitrary"),
        name="sb_attention",
    )(proj, proj, proj)


def _dilated_kernel(q_ref, k_ref, v_ref, cos_ref, sin_ref, o_ref, q_s, k_s, v_s, og_s, lg_s, *, seq):
    for c in range(seq // ROWS_DIL):
        rows = slice(c * ROWS_DIL, (c + 1) * ROWS_DIL)
        cos = cos_ref[rows, :]
        sin = sin_ref[rows, :]
        q_s[rows, :] = _rope128(q_ref[rows, :].astype(F32), cos, sin)
        k_s[rows, :] = _rope128(k_ref[rows, :].astype(F32), cos, sin)
        v_s[rows, :] = v_ref[rows, :].astype(F32)

    ones = jnp.ones((2 * BAND, HEAD_DIM), BF16)
    row1 = lax.broadcasted_iota(jnp.int32, (BAND, BAND), 0)
    col1 = lax.broadcasted_iota(jnp.int32, (BAND, BAND), 1)
    valid1 = col1 <= row1
    row2 = lax.broadcasted_iota(jnp.int32, (BAND, 2 * BAND), 0)
    col2 = lax.broadcasted_iota(jnp.int32, (BAND, 2 * BAND), 1)
    dist = BAND + row2 - col2
    valid2 = (dist >= 0) & (dist <= BAND)

    def attend(g, d, q_start, k_start, n_keys, valid):
        def strided(start, n):
            return pl.ds(start, n, stride=d) if d > 1 else pl.ds(start, n)
        qb = q_s[strided(q_start, BAND), :].astype(BF16)
        kb = k_s[strided(k_start, n_keys), :].astype(BF16)
        vb = v_s[strided(k_start, n_keys), :].astype(BF16)
        s = jnp.where(valid, _dot_nt(qb, kb), NEG_INF)
        m = jnp.max(s, axis=-1, keepdims=True)
        p = jnp.exp2(s - m)
        pv = _dot(p.astype(BF16), jnp.concatenate([vb, ones[:n_keys]], axis=1))
        den = pv[:, HEAD_DIM:]
        og_s[g, strided(q_start, BAND), :] = pv[:, :HEAD_DIM] / den
        lg_s[g, strided(q_start, BAND), :] = m + jnp.log2(den)

    for g, d in enumerate(DILATIONS):
        nb = seq // d // BAND

        def residue(r, carry, g=g, d=d, nb=nb):
            attend(g, d, r, r, BAND, valid1)

            def block(i, c2):
                attend(g, d, r + i * (BAND * d), r + (i - 1) * (BAND * d), 2 * BAND, valid2)
                return c2

            lax.fori_loop(1, nb, block, 0, unroll=DIL_UNROLL if nb > 2 else 1)
            return carry

        lax.fori_loop(0, d, residue, 0, unroll=max(1, min(d, 2 * DIL_UNROLL // nb)))

    for c in range(seq // ROWS_DIL):
        rows = slice(c * ROWS_DIL, (c + 1) * ROWS_DIL)
        l1, l2, l3 = lg_s[0, rows, :], lg_s[1, rows, :], lg_s[2, rows, :]
        top = jnp.maximum(jnp.maximum(l1, l2), l3)
        e1, e2, e3 = jnp.exp2(l1 - top), jnp.exp2(l2 - top), jnp.exp2(l3 - top)
        num = (e1 * og_s[0, rows, :] + e2 * og_s[1, rows, :]) + e3 * og_s[2, rows, :]
        o_ref[rows, :] = (num / ((e1 + e2) + e3)).astype(o_ref.dtype)


def _dilated_attention(proj, cos_b, sin_b, batch, seq):
    m = proj.shape[0]
    blk = (seq, HEAD_DIM)
    n_groups = len(DILATIONS)
    return pl.pallas_call(
        functools.partial(_dilated_kernel, seq=seq),
        grid=(batch, N_HEADS),
        in_specs=[
            pl.BlockSpec(blk, lambda b, h: (b, COL_QB + h)),
            pl.BlockSpec(blk, lambda b, h: (b, COL_KB + h)),
            pl.BlockSpec(blk, lambda b, h: (b, COL_VB + h)),
            pl.BlockSpec(blk, lambda b, h: (0, 0)),
            pl.BlockSpec(blk, lambda b, h: (0, 0)),
        ],
        out_specs=pl.BlockSpec(blk, lambda b, h: (b, h)),
        out_shape=jax.ShapeDtypeStruct((m, GROUP_WIDTH), BF16),
        scratch_shapes=[pltpu.VMEM(blk, F32), pltpu.VMEM(blk, F32), pltpu.VMEM(blk, F32),
                        pltpu.VMEM((n_groups,) + blk, F32), pltpu.VMEM((n_groups,) + blk, F32)],
        compiler_params=_params("parallel", "parallel"),
        name="dilated_attention",
    )(proj, proj, proj, cos_b, sin_b)


def _out_proj_kernel(oa_ref, ob_ref, oc_ref, od_ref, gn_ref, w_ref, x_ref, o_ref):
    for sub in range(TM_OUT // ROWS_OUT):
        rows = slice(sub * ROWS_OUT, (sub + 1) * ROWS_OUT)
        parts = []
        for group, ref in enumerate((oa_ref, ob_ref, oc_ref, od_ref)):
            cols = slice(group * GROUP_WIDTH, (group + 1) * GROUP_WIDTH)
            parts.append(_rms(ref[rows, :].astype(F32), gn_ref[:, cols]).astype(BF16))
        g = jnp.concatenate(parts, axis=1)
        o_ref[rows, :] = x_ref[rows, :] + _dot(g, w_ref[...])


def _out_proj(oa, ob, oc, od, gn, w_out, layer, x):
    m = x.shape[0]
    grp = pl.BlockSpec((TM_OUT, GROUP_WIDTH), lambda i: (i, 0))
    return pl.pallas_call(
        _out_proj_kernel,
        grid=(m // TM_OUT,),
        in_specs=[
            grp, grp, grp, grp,
            pl.BlockSpec((1, D_MODEL), lambda i: (0, 0)),
            pl.BlockSpec((None, D_MODEL, D_MODEL), lambda i: (layer, 0, 0)),
            pl.BlockSpec((TM_OUT, D_MODEL), lambda i: (i, 0)),
        ],
        out_specs=pl.BlockSpec((TM_OUT, D_MODEL), lambda i: (i, 0)),
        out_shape=jax.ShapeDtypeStruct((m, D_MODEL), F32),
        compiler_params=_params("parallel"),
        name="out_proj",
    )(oa, ob, oc, od, gn, w_out, x)


def _ffn_kernel(x_ref, g_ref, wg_ref, wu_ref, wd_ref, fg_ref, o_ref, h_s, *, final):
    f = pl.program_id(1)

    @pl.when(f == 0)
    def _():
        x = x_ref[...]
        h_s[...] = _rms(x, g_ref[...]).astype(BF16)
        o_ref[...] = x

    h = h_s[...]
    gate = _dot(h, wg_ref[...])
    up = _dot(h, wu_ref[...])
    act = (gate / (1.0 + jnp.exp(-gate))) * up
    o_ref[...] += _dot(act.astype(BF16), wd_ref[...])

    if final:
        @pl.when(f == pl.num_programs(1) - 1)
        def _():
            o_ref[...] = _rms(o_ref[...], fg_ref[...])


def _ffn(x, gain, w_gate, w_up, w_down, layer, final_gain, final):
    m = x.shape[0]
    return pl.pallas_call(
        functools.partial(_ffn_kernel, final=final),
        grid=(m // TM_FFN, FFN_HIDDEN // TF_FFN),
        in_specs=[
            pl.BlockSpec((TM_FFN, D_MODEL), lambda i, f: (i, 0)),
            pl.BlockSpec((1, D_MODEL), lambda i, f: (0, 0)),
            pl.BlockSpec((None, D_MODEL, TF_FFN), lambda i, f: (layer, 0, f)),
            pl.BlockSpec((None, D_MODEL, TF_FFN), lambda i, f: (layer, 0, f)),
            pl.BlockSpec((None, TF_FFN, D_MODEL), lambda i, f: (layer, f, 0)),
            pl.BlockSpec((1, D_MODEL), lambda i, f: (0, 0)),
        ],
        out_specs=pl.BlockSpec((TM_FFN, D_MODEL), lambda i, f: (i, 0)),
        out_shape=jax.ShapeDtypeStruct((m, D_MODEL), F32),
        scratch_shapes=[pltpu.VMEM((TM_FFN, D_MODEL), BF16)],
        compiler_params=_params("parallel", "arbitrary"),
        name="ffn_final" if final else "ffn",
    )(x, gain, w_gate, w_up, w_down, final_gain)


def _rot_cols(w):
    half = w.shape[-1] // 2
    return jnp.concatenate([-w[..., half:], w[..., :half]], axis=-1)


def _w_in_kernel(w_ref, s_ref, main_ref, tail_ref):
    w = w_ref[...] * s_ref[...]
    o = IN_OFFSETS
    main_ref[...] = jnp.concatenate([w[:, :o[2]], w[:, o[3]:o[9]], w[:, o[10]:]], axis=1).astype(BF16)
    k_rope = w[:, o[2]:o[3]]
    pad = jnp.zeros((w.shape[0], HEAD_DIM - N_HEADS), F32)
    tail_ref[...] = jnp.concatenate([k_rope, _rot_cols(k_rope), w[:, o[9]:o[10]], pad], axis=1).astype(BF16)


def _prep_w_in(w_in, col_scale):
    depth, _, width = w_in.shape
    return pl.pallas_call(
        _w_in_kernel,
        grid=(depth, D_MODEL // TK_PREP),
        in_specs=[
            pl.BlockSpec((None, TK_PREP, width), lambda l, i: (l, i, 0)),
            pl.BlockSpec((1, width), lambda l, i: (0, 0)),
        ],
        out_specs=[
            pl.BlockSpec((None, TK_PREP, MAIN_WIDTH), lambda l, i: (l, i, 0)),
            pl.BlockSpec((None, TK_PREP, TAIL_WIDTH), lambda l, i: (l, i, 0)),
        ],
        out_shape=[
            jax.ShapeDtypeStruct((depth, D_MODEL, MAIN_WIDTH), BF16),
            jax.ShapeDtypeStruct((depth, D_MODEL, TAIL_WIDTH), BF16),
        ],
        compiler_params=_params("parallel", "parallel"),
        name="prep_w_in",
    )(w_in, col_scale)


def _prep_weights(w_in, w_uq, w_ukv, fox_forget_bias):
    depth = w_in.shape[0]
    col_scale = np.ones((1, IN_OFFSETS[-1]), np.float32)
    for q_split in (3, 6, 10):
        col_scale[:, IN_OFFSETS[q_split]:IN_OFFSETS[q_split + 1]] = HEAD_DIM ** -0.5 * LOG2E
    w_main, w_tail = _prep_w_in(w_in, jnp.asarray(col_scale))
    uq = (w_uq * (MLA_QK_DIM ** -0.5 * LOG2E)).reshape(depth, Q_LORA, N_HEADS, MLA_QK_DIM)
    pe = uq[..., QK_NOPE:]
    uq = jnp.concatenate([uq[..., :QK_NOPE], pe, _rot_cols(pe)], axis=-1)
    uq = uq.reshape(depth, Q_LORA, N_HEADS * MLA_HEAD_PAD)
    ukv = w_ukv.reshape(depth, KV_LORA, N_HEADS, 2, HEAD_DIM).transpose(0, 1, 3, 2, 4)
    ukv = ukv.reshape(depth, KV_LORA, 2 * GROUP_WIDTH)
    bias_rows = jnp.pad(fox_forget_bias, ((0, 0), (0, HEAD_DIM - N_HEADS)))[:, None, :]
    return w_main, w_tail, uq.astype(BF16), ukv.astype(BF16), bias_rows


def _rope_tables(seq):
    pos = jnp.arange(seq, dtype=F32)[:, None]

    def angles(dim):
        inv_freq = ROPE_THETA ** (-jnp.arange(0, dim, 2, dtype=F32) / dim)
        return pos * inv_freq[None, :]

    ang_b = angles(HEAD_DIM)
    cos_b = jnp.concatenate([jnp.cos(ang_b)] * 2, axis=1)
    sin_b = jnp.concatenate([-jnp.sin(ang_b), jnp.sin(ang_b)], axis=1)
    ang_a = angles(QK_ROPE)
    zeros = jnp.zeros((seq, HEAD_DIM - QK_ROPE), F32)
    cos_a = jnp.concatenate([jnp.cos(ang_a)] * 2 + [zeros], axis=1)
    sin_a = jnp.concatenate([jnp.sin(ang_a)] * 2 + [zeros], axis=1)
    return cos_a, sin_a, cos_b, sin_b


def kernel(x, attn_norm, w_in, mla_q_norm, w_uq, mla_kv_norm, w_ukv, fox_forget_bias, group_norm,
           w_out, ffn_norm, w_gate, w_up, w_down, final_norm):
    batch, seq, _ = x.shape
    m = batch * seq
    assert seq % TQ_ATT == 0 and seq % (max(DILATIONS) * BAND) == 0 and m % TM_IN == 0
    w_main, w_tail, uq, ukv, bias_rows = _prep_weights(w_in, w_uq, w_ukv, fox_forget_bias)
    w_out_b, w_gate_b, w_up_b, w_down_b = (w.astype(BF16) for w in (w_out, w_gate, w_up, w_down))
    cos_a, sin_a, cos_b, sin_b = _rope_tables(seq)
    row = lambda v: v.reshape(1, -1)

    xf = x.reshape(m, D_MODEL)
    for l in range(DEPTH):
        proj, tail = _in_proj(xf, row(attn_norm[l]), w_main, w_tail, l)
        q_a, kv_a, kpe = _mla_prep(proj, tail, row(mla_q_norm[l]), row(mla_kv_norm[l]), uq, ukv, l,
                                   cos_a, sin_a, seq)
        ccol, crow = _fox_prep(tail, bias_rows[l], batch, seq)
        out_a = _mla_attention(q_a, kv_a, kpe, batch, seq)
        out_b = _dilated_attention(proj, cos_b, sin_b, batch, seq)
        out_c = _fox_attention(proj, ccol, crow, batch, seq)
        out_d = _sb_attention(proj, batch, seq)
        xf = _out_proj(out_a, out_b, out_c, out_d, row(group_norm[l]), w_out_b, l, xf)
        xf = _ffn(xf, row(ffn_norm[l]), w_gate_b, w_up_b, w_down_b, l, row(final_norm),
                  final=(l == DEPTH - 1))
    return xf.reshape(batch, seq, D_MODEL)
```
